```python
import functools
import jax, jax.numpy as jnp
from jax import lax
import numpy as np

D_MODEL = 1024
BATCH = 4
SEQ = 4096
DEPTH = 1
DEC_BATCH = 128
DEC_SEQ = 1
PAST_LEN = 8192
PAGE_SIZE = 128

N_HEADS = 8
N_KV_HEADS = 2
HEAD_DIM = 64
GQA_GROUP = N_HEADS // N_KV_HEADS
ATTN_WIDTH = N_HEADS * HEAD_DIM
KV_WIDTH = N_KV_HEADS * HEAD_DIM
WINDOW = 128
BLOCK = WINDOW
ATTN_SCALE = HEAD_DIM ** -0.5
NEG_INF = -1e30
POOL_WINDOWS = (2, 4, 8, 16)
N_POOL_GROUPS = len(POOL_WINDOWS)
POOL_WIDTH = D_MODEL - ATTN_WIDTH
POOL_GROUP_DIM = POOL_WIDTH // N_POOL_GROUPS
POOL_HIST = max(POOL_WINDOWS) - 1
MIX_WIDTH = ATTN_WIDTH + POOL_WIDTH
IN_WIDTH = ATTN_WIDTH + 2 * KV_WIDTH + POOL_WIDTH
D_FF = -(-8 * D_MODEL // (3 * 256)) * 256
RMS_EPS = 1e-5

kernel_name = "hymba_swa_sink_alibi_pool_swiglu_step"


def rms_norm(x, g):
    x32 = x.astype(jnp.float32)
    y = x32 * lax.rsqrt(jnp.mean(x32 * x32, axis=-1, keepdims=True) + RMS_EPS)
    return (y * g.astype(jnp.float32)).astype(x.dtype)


def alibi_slopes():
    return jnp.exp2(-8.0 * jnp.arange(1, N_HEADS + 1, dtype=jnp.float32) / N_HEADS)


def attend(q, kk, vv, rel, valid, sinks):
    scores = jnp.einsum('...qhgd,...khd->...hgqk', q, kk,
                        preferred_element_type=jnp.float32) * ATTN_SCALE
    slopes = alibi_slopes().reshape(N_KV_HEADS, GQA_GROUP)[:, :, None, None]
    scores = scores - slopes * rel.astype(jnp.float32)[..., None, None, :, :]
    scores = jnp.where(valid[..., None, None, :, :], scores, NEG_INF)
    sink = sinks.astype(jnp.float32).reshape(N_KV_HEADS, GQA_GROUP)[:, :, None, None]
    sink = jnp.broadcast_to(sink, scores.shape[:-1] + (1,))
    probs = jax.nn.softmax(jnp.concatenate([scores, sink], axis=-1), axis=-1)[..., :-1]
    return jnp.einsum('...hgqk,...khd->...qhgd', probs.astype(vv.dtype), vv)


def banded_attention(q, k, v, sinks):
    B, S = q.shape[:2]
    nb = S // BLOCK
    qb = q.reshape(B, nb, BLOCK, N_KV_HEADS, GQA_GROUP, HEAD_DIM)
    kb = k.reshape(B, nb, BLOCK, N_KV_HEADS, HEAD_DIM)
    vb = v.reshape(B, nb, BLOCK, N_KV_HEADS, HEAD_DIM)

    def with_prev(xb):
        prev = jnp.pad(xb, ((0, 0), (1, 0), (0, 0), (0, 0), (0, 0)))[:, :-1]
        return jnp.concatenate([prev, xb], axis=2)

    qi = jnp.arange(BLOCK)[:, None]
    ki = jnp.arange(2 * BLOCK)[None, :]
    rel = BLOCK + qi - ki
    key_pos = (jnp.arange(nb)[:, None, None] - 1) * BLOCK + ki[None]
    valid = (rel >= 0) & (rel <= WINDOW) & (key_pos >= 0)
    out = attend(qb, with_prev(kb), with_prev(vb), rel, valid, sinks)
    return out.reshape(B, S, ATTN_WIDTH)


def buffered_attention(q, k_new, v_new, buf_k, buf_v, sinks, pos0):
    B, T = q.shape[:2]
    L = buf_k.shape[1]
    kk = jnp.concatenate([buf_k, k_new.astype(buf_k.dtype)], axis=1)
    vv = jnp.concatenate([buf_v, v_new.astype(buf_v.dtype)], axis=1)
    qi = jnp.arange(T)[:, None]
    ki = jnp.arange(L + T)[None, :]
    rel = L + qi - ki
    key_pos = pos0 - L + ki
    valid = (rel >= 0) & (rel <= WINDOW) & (key_pos >= 0)
    out = attend(q, kk, vv, rel, valid, sinks).reshape(B, T, ATTN_WIDTH)
    return out, kk[:, -L:], vv[:, -L:]


def pool_mix(hist, u, pos0, w_pool, pool_scale):
    B, T, C = u.shape
    Lh = hist.shape[1]
    z = jnp.concatenate([hist.astype(u.dtype), u], axis=1).astype(jnp.float32)
    row_pos = pos0 - Lh + jnp.arange(Lh + T)
    z = jnp.where((row_pos >= 0)[None, :, None], z, 0.0)
    cz = jnp.concatenate([jnp.zeros((B, 1, C), jnp.float32), jnp.cumsum(z, axis=1)], axis=1)
    pos = pos0 + jnp.arange(T)
    outs = []
    for g, w in enumerate(POOL_WINDOWS):
        sl = slice(g * POOL_GROUP_DIM, (g + 1) * POOL_GROUP_DIM)
        win_sum = cz[:, Lh + 1:Lh + 1 + T, sl] - cz[:, Lh + 1 - w:Lh + 1 - w + T, sl]
        count = jnp.minimum(w, pos + 1).astype(jnp.float32)[None, :, None]
        outs.append(win_sum / count - z[:, Lh:, sl])
    m = jnp.stack(outs, axis=2).astype(u.dtype)
    y = jnp.einsum('btgc,gcd->btgd', m, w_pool).reshape(B, T, POOL_WIDTH) * pool_scale
    new_hist = jnp.concatenate([hist.astype(u.dtype), u], axis=1)[:, -Lh:]
    return y, new_hist


def split_proj(proj):
    B, T = proj.shape[:2]
    q = proj[..., :ATTN_WIDTH].reshape(B, T, N_KV_HEADS, GQA_GROUP, HEAD_DIM)
    k = proj[..., ATTN_WIDTH:ATTN_WIDTH + KV_WIDTH].reshape(B, T, N_KV_HEADS, HEAD_DIM)
    v = proj[..., ATTN_WIDTH + KV_WIDTH:ATTN_WIDTH + 2 * KV_WIDTH].reshape(B, T, N_KV_HEADS, HEAD_DIM)
    u = proj[..., ATTN_WIDTH + 2 * KV_WIDTH:]
    return q, k, v, u


def prompt_mixer(q, k, v, u, sinks, w_pool, pool_scale):
    B, S = u.shape[:2]
    a = banded_attention(q, k, v, sinks)
    hist = jnp.zeros((B, POOL_HIST, POOL_WIDTH), u.dtype)
    p, new_pool = pool_mix(hist, u, 0, w_pool, pool_scale)
    n_keep = min(WINDOW, S)
    return jnp.concatenate([a, p], axis=-1), (k[:, -n_keep:], v[:, -n_keep:], new_pool)


def sample_mixer(q, k, v, u, buf_k, buf_v, hist, sinks, w_pool, pool_scale):
    a, new_k, new_v = buffered_attention(q, k, v, buf_k, buf_v, sinks, PAST_LEN)
    p, new_pool = pool_mix(hist, u, PAST_LEN, w_pool, pool_scale)
    return jnp.concatenate([a, p], axis=-1), (new_k, new_v, new_pool)


def decoder_layer(x, mix, norm1, w_in, w_out, norm2, w_gate, w_up, w_down):
    h = rms_norm(x, norm1)
    mixed, state = mix(*split_proj(h @ w_in))
    x = x + mixed @ w_out
    h = rms_norm(x, norm2)
    x = x + (jax.nn.silu(h @ w_gate) * (h @ w_up)) @ w_down
    return x, state


def setup_inputs(seed: int = 0) -> dict:
    key = jax.random.key(seed)
    ks = jax.random.split(key, 18)
    wbuf = min(WINDOW, PAST_LEN)
    f32 = jnp.float32
    nrm = lambda k, shape, s=1.0: jax.random.normal(k, shape, f32) * s
    return {
        "x_prompt": nrm(ks[0], (BATCH, SEQ, D_MODEL)),
        "x_sample": nrm(ks[1], (DEC_BATCH, DEC_SEQ, D_MODEL)),
        "cache_k_window": nrm(ks[2], (DEPTH, DEC_BATCH, wbuf, N_KV_HEADS, HEAD_DIM)),
        "cache_v_window": nrm(ks[3], (DEPTH, DEC_BATCH, wbuf, N_KV_HEADS, HEAD_DIM)),
        "state_pool": nrm(ks[4], (DEPTH, DEC_BATCH, POOL_HIST, POOL_WIDTH)),
        "norm1": 1.0 + nrm(ks[5], (DEPTH, D_MODEL), 0.05),
        "w_in": nrm(ks[6], (DEPTH, D_MODEL, IN_WIDTH), D_MODEL ** -0.5),
        "attn_sinks": nrm(ks[7], (DEPTH, N_HEADS), 0.5),
        "w_pool": nrm(ks[8], (DEPTH, N_POOL_GROUPS, POOL_GROUP_DIM, POOL_GROUP_DIM), POOL_GROUP_DIM ** -0.5),
        "pool_scale": 1.0 + nrm(ks[9], (DEPTH, POOL_WIDTH), 0.1),
        "w_out": nrm(ks[10], (DEPTH, MIX_WIDTH, D_MODEL), MIX_WIDTH ** -0.5),
        "norm2": 1.0 + nrm(ks[11], (DEPTH, D_MODEL), 0.05),
        "w_gate": nrm(ks[12], (DEPTH, D_MODEL, D_FF), D_MODEL ** -0.5),
        "w_up": nrm(ks[13], (DEPTH, D_MODEL, D_FF), D_MODEL ** -0.5),
        "w_down": nrm(ks[14], (DEPTH, D_FF, D_MODEL), D_FF ** -0.5),
        "final_norm": 1.0 + nrm(ks[15], (D_MODEL,), 0.05),
    }


def reference(x_prompt, x_sample, cache_k_window, cache_v_window, state_pool,
              norm1, w_in, attn_sinks, w_pool, pool_scale, w_out, norm2,
              w_gate, w_up, w_down, final_norm):
    xp, xs = x_prompt, x_sample
    kp_list, vp_list, pp_list, ks_list, vs_list, ps_list = [], [], [], [], [], []
    for l in range(DEPTH):
        ffn = dict(norm1=norm1[l], w_in=w_in[l], w_out=w_out[l], norm2=norm2[l],
                   w_gate=w_gate[l], w_up=w_up[l], w_down=w_down[l])
        pmix = functools.partial(prompt_mixer, sinks=attn_sinks[l], w_pool=w_pool[l],
                                 pool_scale=pool_scale[l])
        smix = functools.partial(sample_mixer, buf_k=cache_k_window[l], buf_v=cache_v_window[l],
                                 hist=state_pool[l], sinks=attn_sinks[l], w_pool=w_pool[l],
                                 pool_scale=pool_scale[l])
        xp, (kp, vp, pp) = decoder_layer(xp, pmix, **ffn)
        xs, (ks_, vs_, ps_) = decoder_layer(xs, smix, **ffn)
        kp_list.append(kp); vp_list.append(vp); pp_list.append(pp)
        ks_list.append(ks_); vs_list.append(vs_); ps_list.append(ps_)
    y_prompt = rms_norm(xp, final_norm)
    y_sample = rms_norm(xs, final_norm)
    new_k_prompt = jnp.stack(kp_list, axis=0)
    new_v_prompt = jnp.stack(vp_list, axis=0)
    new_pool_prompt = jnp.stack(pp_list, axis=0)
    new_k_sample = jnp.stack(ks_list, axis=0)
    new_v_sample = jnp.stack(vs_list, axis=0)
    new_pool_sample = jnp.stack(ps_list, axis=0)
    return (y_prompt, y_sample, new_k_prompt, new_v_prompt, new_pool_prompt,
            new_k_sample, new_v_sample, new_pool_sample)
```

```python
import functools

import numpy as np
import jax
import jax.numpy as jnp
from jax import lax
from jax.experimental import pallas as pl
from jax.experimental.pallas import tpu as pltpu

N_HEADS = 8
N_KV_HEADS = 2
HEAD_DIM = 64
GQA_GROUP = N_HEADS // N_KV_HEADS
WINDOW = 128
BLOCK = WINDOW
PAST_LEN = 8192
POOL_WINDOWS = (2, 4, 8, 16)
POOL_HIST = max(POOL_WINDOWS) - 1
RMS_EPS = 1e-5
NEG_INF = -1e30
ATTN_SCALE = HEAD_DIM ** -0.5

ATTN_WIDTH = N_HEADS * HEAD_DIM
KV_WIDTH = N_KV_HEADS * HEAD_DIM
N_PAIRS = N_HEADS // 2
LANES = 128
SUBLANES = 8
HIST_PAD = 16

PROMPT_TILE = 512
SAMPLE_BLOCK = 32
FFN_CHUNK = 1024
VMEM_LIMIT_BYTES = 60 * 1024 * 1024

F32 = jnp.float32
BF16 = jnp.bfloat16


def _rms(x, g):
    var = jnp.mean(x * x, axis=-1, keepdims=True)
    return x * lax.rsqrt(var + RMS_EPS) * g


def _dot(a, b):
    return jnp.dot(a, b, preferred_element_type=F32)


def _dot_nt(a, b):
    return lax.dot_general(a, b, (((1,), (1,)), ((), ())), preferred_element_type=F32)


def _ffn_chunks(d_ff):
    return [(c, min(c + FFN_CHUNK, d_ff)) for c in range(0, d_ff, FFN_CHUNK)]


def _dense_tail(x, mixed_bf16, wout_ref, n2_ref, wg_ref, wu_ref, wd_ref, fn_ref, act_ref):
    x1 = x + _dot(mixed_bf16, wout_ref[...])
    h2 = _rms(x1, n2_ref[...]).astype(BF16)
    for c0, c1 in _ffn_chunks(wg_ref.shape[1]):
        gate = _dot(h2, wg_ref[:, c0:c1])
        up = _dot(h2, wu_ref[:, c0:c1])
        act_ref[:, c0:c1] = (jax.nn.silu(gate) * up).astype(BF16)
    x2 = x1 + _dot(act_ref[...], wd_ref[...])
    return _rms(x2, fn_ref[...])


def _split_heads_to_lane_halves(a):
    lo = lax.broadcasted_iota(jnp.int32, a.shape, 1) < HEAD_DIM
    zero = jnp.zeros_like(a)
    ar = pltpu.roll(a, HEAD_DIM, axis=1)
    return (jnp.where(lo, a, zero).astype(BF16), jnp.where(lo, zero, ar).astype(BF16),
            jnp.where(lo, ar, zero).astype(BF16), jnp.where(lo, zero, a).astype(BF16))


def _prompt_body(x_ref, n1_ref, win_ref, bias_ref, sink_ref, wpool_ref, pscale_ref, wout_ref, n2_ref,
                 wg_ref, wu_ref, wd_ref, fn_ref,
                 y_ref, kout_ref, vout_ref, pout_ref,
                 kk_ref, vv_ref, ubuf_ref, mixed_ref, act_ref):
    T = x_ref.shape[1]
    t = pl.program_id(1)

    @pl.when(t == 0)
    def _():
        kk_ref[:, 0:BLOCK, :] = jnp.zeros((4, BLOCK, LANES), BF16)
        vv_ref[:, 0:BLOCK, :] = jnp.zeros((4, BLOCK, LANES), BF16)
        ubuf_ref[0:HIST_PAD, :] = jnp.zeros((HIST_PAD, ubuf_ref.shape[1]), F32)

    x = x_ref[0]
    h = _rms(x, n1_ref[...]).astype(BF16)
    proj = _dot(h, win_ref[...])
    q = (proj[:, 0:ATTN_WIDTH] * ATTN_SCALE).astype(BF16)
    k = proj[:, ATTN_WIDTH:ATTN_WIDTH + KV_WIDTH]
    v = proj[:, ATTN_WIDTH + KV_WIDTH:ATTN_WIDTH + 2 * KV_WIDTH]
    u = proj[:, ATTN_WIDTH + 2 * KV_WIDTH:]

    kout_ref[0] = k[T - WINDOW:, :]
    vout_ref[0] = v[T - WINDOW:, :]
    ubuf_ref[HIST_PAD:, :] = u
    pout_ref[0] = ubuf_ref[HIST_PAD + T - POOL_HIST:HIST_PAD + T, :]

    for idx, part in enumerate(_split_heads_to_lane_halves(k)):
        kk_ref[idx, BLOCK:, :] = part
    for idx, part in enumerate(_split_heads_to_lane_halves(v)):
        vv_ref[idx, BLOCK:, :] = part

    lo = lax.broadcasted_iota(jnp.int32, (BLOCK, LANES), 1) < HEAD_DIM
    first_tile = (t == 0).astype(jnp.int32)
    for j in range(T // BLOCK):
        bias_idx = first_tile if j == 0 else 0
        rows = slice(j * BLOCK, (j + 1) * BLOCK)
        keys = slice(j * BLOCK, j * BLOCK + 2 * BLOCK)
        for p in range(N_PAIRS):
            g = (2 * p) // GQA_GROUP
            cols = slice(p * LANES, (p + 1) * LANES)
            kmat = jnp.concatenate([kk_ref[2 * g, keys, :], kk_ref[2 * g + 1, keys, :]], axis=0)
            s = _dot_nt(q[rows, cols], kmat) + bias_ref[bias_idx, p]
            probs, denoms = [], []
            for hh in range(2):
                sh = s[:, hh * 2 * BLOCK:(hh + 1) * 2 * BLOCK]
                sink = sink_ref[2 * p + hh]
                m = jnp.maximum(jnp.max(sh, axis=-1, keepdims=True), sink)
                e = jnp.exp(sh - m)
                denoms.append(jnp.sum(e, axis=-1, keepdims=True) + jnp.exp(sink - m))
                probs.append(e.astype(BF16))
            vmat = jnp.concatenate([vv_ref[2 * g, keys, :], vv_ref[2 * g + 1, keys, :]], axis=0)
            o = _dot(jnp.concatenate(probs, axis=1), vmat)
            denom = jnp.where(lo, denoms[0], denoms[1])
            mixed_ref[rows, cols] = (o / denom).astype(BF16)

    pos = lax.broadcasted_iota(jnp.int32, (T, 1), 0) + t * T
    for g, w in enumerate(POOL_WINDOWS):
        cols = slice(g * LANES, (g + 1) * LANES)
        cur = ubuf_ref[HIST_PAD:HIST_PAD + T, cols]
        win = cur
        for back in range(1, w):
            win = win + ubuf_ref[HIST_PAD - back:HIST_PAD - back + T, cols]
        count = jnp.minimum(w, pos + 1).astype(F32)
        m = (win / count - cur).astype(BF16)
        yg = _dot(m, wpool_ref[g]) * pscale_ref[:, cols]
        mixed_ref[:, ATTN_WIDTH + g * LANES:ATTN_WIDTH + (g + 1) * LANES] = yg.astype(BF16)

    kk_ref[:, 0:BLOCK, :] = kk_ref[:, T:T + BLOCK, :]
    vv_ref[:, 0:BLOCK, :] = vv_ref[:, T:T + BLOCK, :]
    ubuf_ref[0:HIST_PAD, :] = ubuf_ref[T:T + HIST_PAD, :]

    y_ref[0] = _dense_tail(x, mixed_ref[...], wout_ref, n2_ref, wg_ref, wu_ref, wd_ref, fn_ref, act_ref)


def _sample_body(xs_ref, n1_ref, wins_ref, kc_ref, vc_ref, hist_ref, bias_ref, sinkrow_ref,
                 wpool_ref, pscale_ref, wout_ref, n2_ref, wg_ref, wu_ref, wd_ref, fn_ref,
                 ys_ref, nk_ref, nv_ref, npool_ref,
                 proj_ref, mixed_ref, act_ref):
    SB = kc_ref.shape[0]
    L = kc_ref.shape[1]
    i = pl.program_id(0)
    q_width = N_HEADS * LANES

    @pl.when(i == 0)
    def _():
        h = _rms(xs_ref[...], n1_ref[...]).astype(BF16)
        proj_ref[...] = _dot(h, wins_ref[...])

    r0 = pl.multiple_of(i * SB, SB)
    pr = proj_ref[pl.ds(r0, SB), :]
    qs = pr[:, 0:q_width] * ATTN_SCALE
    knew = pr[:, q_width:q_width + KV_WIDTH]
    vnew = pr[:, q_width + KV_WIDTH:q_width + 2 * KV_WIDTH]
    u = pr[:, q_width + 2 * KV_WIDTH:]

    nk_ref[:, 0:L - 1, :] = kc_ref[:, 1:L, :]
    nk_ref[:, L - 1, :] = knew
    nv_ref[:, 0:L - 1, :] = vc_ref[:, 1:L, :]
    nv_ref[:, L - 1, :] = vnew
    npool_ref[:, 0:POOL_HIST - 1, :] = hist_ref[:, 1:POOL_HIST, :]
    npool_ref[:, POOL_HIST - 1, :] = u

    rows64 = N_HEADS * SUBLANES
    sub = lax.broadcasted_iota(jnp.int32, (rows64, LANES), 0) % SUBLANES
    lo = lax.broadcasted_iota(jnp.int32, (SUBLANES, LANES), 1) < HEAD_DIM
    bias = bias_ref[...]
    sink = sinkrow_ref[...]
    for r in range(SB // SUBLANES):
        grp = slice(r * SUBLANES, (r + 1) * SUBLANES)
        lhs = jnp.concatenate([qs[grp, h * LANES:(h + 1) * LANES] for h in range(N_HEADS)], axis=0)
        lhs_bf = lhs.astype(BF16)
        s = jnp.zeros((rows64, L), F32)
        for si in range(SUBLANES):
            kb = kc_ref[r * SUBLANES + si].astype(BF16)
            s = jnp.where(sub == si, _dot_nt(lhs_bf, kb), s)
        s = s + bias
        knew_rep = jnp.concatenate([knew[grp]] * N_HEADS, axis=0)
        vnew_rep = jnp.concatenate([vnew[grp]] * N_HEADS, axis=0)
        s_new = jnp.sum(lhs * knew_rep, axis=-1, keepdims=True)
        m = jnp.maximum(jnp.maximum(jnp.max(s, axis=-1, keepdims=True), s_new), sink)
        e = jnp.exp(s - m)
        e_new = jnp.exp(s_new - m)
        denom = jnp.sum(e, axis=-1, keepdims=True) + e_new + jnp.exp(sink - m)
        e_bf = e.astype(BF16)
        o = jnp.zeros((rows64, LANES), F32)
        for si in range(SUBLANES):
            vb = vc_ref[r * SUBLANES + si].astype(BF16)
            o = jnp.where(sub == si, _dot(e_bf, vb), o)
        o = (o + e_new * vnew_rep) / denom
        out_rows = pl.ds(pl.multiple_of(r0 + r * SUBLANES, SUBLANES), SUBLANES)
        for p in range(N_PAIRS):
            a = o[2 * p * SUBLANES:(2 * p + 1) * SUBLANES]
            b = o[(2 * p + 1) * SUBLANES:(2 * p + 2) * SUBLANES]
            if (2 * p) // GQA_GROUP == 0:
                pair = jnp.where(lo, a, pltpu.roll(b, HEAD_DIM, axis=1))
            else:
                pair = jnp.where(lo, pltpu.roll(a, HEAD_DIM, axis=1), b)
            mixed_ref[out_rows, p * LANES:(p + 1) * LANES] = pair

    blk_rows = pl.ds(r0, SB)
    for g, w in enumerate(POOL_WINDOWS):
        cols = slice(g * LANES, (g + 1) * LANES)
        cur = u[:, cols]
        win = cur
        for back in range(1, w):
            win = win + hist_ref[:, POOL_HIST - back, cols]
        count = float(min(w, PAST_LEN + 1))
        m = (win / count - cur).astype(BF16)
        yg = _dot(m, wpool_ref[g]) * pscale_ref[:, cols]
        mixed_ref[blk_rows, ATTN_WIDTH + g * LANES:ATTN_WIDTH + (g + 1) * LANES] = yg

    @pl.when(i == pl.num_programs(0) - 1)
    def _():
        ys_ref[...] = _dense_tail(xs_ref[...], mixed_ref[...].astype(BF16), wout_ref, n2_ref,
                                  wg_ref, wu_ref, wd_ref, fn_ref, act_ref)


def _alibi_slopes():
    return np.exp2(-8.0 * np.arange(1, N_HEADS + 1, dtype=np.float32) / N_HEADS).astype(np.float32)


def _prompt_bias():
    slopes = _alibi_slopes()
    qi = np.arange(BLOCK)[:, None]
    ki = np.arange(2 * BLOCK)[None, :]
    rel = BLOCK + qi - ki
    band = (rel >= 0) & (rel <= WINDOW)
    out = np.empty((2, N_PAIRS, BLOCK, 4 * BLOCK), np.float32)
    for first in range(2):
        valid = band & ((ki >= BLOCK) if first else True)
        for p in range(N_PAIRS):
            for hh in range(2):
                val = np.where(valid, -slopes[2 * p + hh] * rel.astype(np.float32), np.float32(NEG_INF))
                out[first, p, :, hh * 2 * BLOCK:(hh + 1) * 2 * BLOCK] = val
    return out


def _sample_bias(L):
    assert PAST_LEN >= L and L <= WINDOW
    slopes = _alibi_slopes()
    rel = (L - np.arange(L)).astype(np.float32)
    return np.repeat(-slopes[:, None] * rel[None, :], SUBLANES, axis=0).astype(np.float32)


def _resident(shape):
    zeros = (0,) * len(shape)
    return pl.BlockSpec(shape, lambda *_: zeros, pipeline_mode=pl.Buffered(1))


def kernel(x_prompt, x_sample, cache_k_window, cache_v_window, state_pool, norm1, w_in, attn_sinks, w_pool,
           pool_scale, w_out, norm2, w_gate, w_up, w_down, final_norm):
    B, S, D = x_prompt.shape
    NS = x_sample.shape[0]
    L = cache_k_window.shape[2]
    d_ff = w_gate.shape[-1]
    pool_w = state_pool.shape[-1]
    assert norm1.shape[0] == 1 and x_sample.shape[1] == 1
    assert w_in.shape[-1] == ATTN_WIDTH + 2 * KV_WIDTH + pool_w and pool_w == len(POOL_WINDOWS) * LANES
    assert S % PROMPT_TILE == 0 and NS % SAMPLE_BLOCK == 0 and state_pool.shape[2] == POOL_HIST
    T = PROMPT_TILE

    w_in_bf = w_in[0].astype(BF16)
    w_pool_bf = w_pool[0].astype(BF16)
    w_out_bf = w_out[0].astype(BF16)
    w_gate_bf = w_gate[0].astype(BF16)
    w_up_bf = w_up[0].astype(BF16)
    w_down_bf = w_down[0].astype(BF16)
    n1 = norm1[0].reshape(1, D)
    n2 = norm2[0].reshape(1, D)
    fn = final_norm.reshape(1, D)
    pscale = pool_scale[0].reshape(1, pool_w)
    sinks = attn_sinks[0]

    mixed_w = ATTN_WIDTH + pool_w
    dense_specs = [_resident((len(POOL_WINDOWS), LANES, LANES)), _resident((1, pool_w)), _resident((mixed_w, D)),
                   _resident((1, D)), _resident((D, d_ff)), _resident((D, d_ff)), _resident((d_ff, D)),
                   _resident((1, D))]
    dense_args = (w_pool_bf, pscale, w_out_bf, n2, w_gate_bf, w_up_bf, w_down_bf, fn)

    y_prompt, k_last, v_last, pool_last = pl.pallas_call(
        _prompt_body,
        grid=(B, S // T),
        in_specs=[pl.BlockSpec((1, T, D), lambda b, t: (b, t, 0)),
                  _resident((1, D)),
                  _resident(w_in_bf.shape),
                  _resident((2, N_PAIRS, BLOCK, 4 * BLOCK)),
                  pl.BlockSpec(memory_space=pltpu.SMEM)] + dense_specs,
        out_specs=[pl.BlockSpec((1, T, D), lambda b, t: (b, t, 0)),
                   pl.BlockSpec((1, WINDOW, KV_WIDTH), lambda b, t: (b, 0, 0)),
                   pl.BlockSpec((1, WINDOW, KV_WIDTH), lambda b, t: (b, 0, 0)),
                   pl.BlockSpec((1, POOL_HIST, pool_w), lambda b, t: (b, 0, 0))],
        out_shape=[jax.ShapeDtypeStruct((B, S, D), F32),
                   jax.ShapeDtypeStruct((B, WINDOW, KV_WIDTH), F32),
                   jax.ShapeDtypeStruct((B, WINDOW, KV_WIDTH), F32),
                   jax.ShapeDtypeStruct((B, POOL_HIST, pool_w), F32)],
        scratch_shapes=[pltpu.VMEM((4, T + BLOCK, LANES), BF16),
                        pltpu.VMEM((4, T + BLOCK, LANES), BF16),
                        pltpu.VMEM((T + HIST_PAD, pool_w), F32),
                        pltpu.VMEM((T, mixed_w), BF16),
                        pltpu.VMEM((T, d_ff), BF16)],
        compiler_params=pltpu.CompilerParams(dimension_semantics=("arbitrary", "arbitrary"),
                                             vmem_limit_bytes=VMEM_LIMIT_BYTES),
        name="prompt_layer",
    )(x_prompt, n1, w_in_bf, jnp.asarray(_prompt_bias()), sinks, *dense_args)

    wq = w_in[0][:, :ATTN_WIDTH].reshape(D, N_HEADS, HEAD_DIM)
    zq = jnp.zeros_like(wq)
    kv_of_head = (np.arange(N_HEADS) // GQA_GROUP)[None, :, None]
    wq_pad = jnp.concatenate([jnp.where(kv_of_head == 0, wq, zq), jnp.where(kv_of_head == 1, wq, zq)], axis=-1)
    w_in_s = jnp.concatenate([wq_pad.reshape(D, N_HEADS * LANES), w_in[0][:, ATTN_WIDTH:]], axis=-1).astype(BF16)
    in_w_s = w_in_s.shape[1]

    SB = SAMPLE_BLOCK
    sink_rows = jnp.repeat(sinks, SUBLANES).reshape(N_HEADS * SUBLANES, 1)
    y_sample, nk, nv, npool = pl.pallas_call(
        _sample_body,
        grid=(NS // SB,),
        in_specs=[_resident((NS, D)),
                  _resident((1, D)),
                  _resident((D, in_w_s)),
                  pl.BlockSpec((SB, L, KV_WIDTH), lambda i: (i, 0, 0)),
                  pl.BlockSpec((SB, L, KV_WIDTH), lambda i: (i, 0, 0)),
                  pl.BlockSpec((SB, POOL_HIST, pool_w), lambda i: (i, 0, 0)),
                  _resident((N_HEADS * SUBLANES, L)),
                  _resident((N_HEADS * SUBLANES, 1))] + dense_specs,
        out_specs=[pl.BlockSpec((NS, D), lambda i: (0, 0)),
                   pl.BlockSpec((SB, L, KV_WIDTH), lambda i: (i, 0, 0)),
                   pl.BlockSpec((SB, L, KV_WIDTH), lambda i: (i, 0, 0)),
                   pl.BlockSpec((SB, POOL_HIST, pool_w), lambda i: (i, 0, 0))],
        out_shape=[jax.ShapeDtypeStruct((NS, D), F32),
                   jax.ShapeDtypeStruct((NS, L, KV_WIDTH), F32),
                   jax.ShapeDtypeStruct((NS, L, KV_WIDTH), F32),
                   jax.ShapeDtypeStruct((NS, POOL_HIST, pool_w), F32)],
        scratch_shapes=[pltpu.VMEM((NS, in_w_s), F32),
                        pltpu.VMEM((NS, mixed_w), F32),
                        pltpu.VMEM((NS, d_ff), BF16)],
        compiler_params=pltpu.CompilerParams(dimension_semantics=("arbitrary",),
                                             vmem_limit_bytes=VMEM_LIMIT_BYTES),
        name="sample_layer",
    )(x_sample.reshape(NS, D), n1, w_in_s, cache_k_window[0].reshape(NS, L, KV_WIDTH),
      cache_v_window[0].reshape(NS, L, KV_WIDTH), state_pool[0], jnp.asarray(_sample_bias(L)), sink_rows,
      *dense_args)

    kv_shape = (1, B, WINDOW, N_KV_HEADS, HEAD_DIM)
    skv_shape = (1, NS, L, N_KV_HEADS, HEAD_DIM)
    return (y_prompt, y_sample.reshape(NS, 1, D), k_last.reshape(kv_shape), v_last.reshape(kv_shape),
            pool_last[None], nk.reshape(skv_shape), nv.reshape(skv_shape), npool[None])
```

```python
import functools

import numpy as np
import jax
import jax.numpy as jnp
from jax import lax
from jax.experimental import pallas as pl
from jax.experimental.pallas import tpu as pltpu

N_HEADS = 8
N_KV_HEADS = 2
HEAD_DIM = 64
GQA_GROUP = N_HEADS // N_KV_HEADS
WINDOW = 128
BLOCK = WINDOW
PAST_LEN = 8192
POOL_WINDOWS = (2, 4, 8, 16)
POOL_HIST = max(POOL_WINDOWS) - 1
RMS_EPS = 1e-5
NEG_INF = -1e30
ATTN_SCALE = HEAD_DIM ** -0.5

ATTN_WIDTH = N_HEADS * HEAD_DIM
KV_WIDTH = N_KV_HEADS * HEAD_DIM
N_PAIRS = N_HEADS // 2
LANES = 128
SUBLANES = 8
HIST_PAD = 16

PROMPT_TILE = 512
SAMPLE_BLOCK = 32
FFN_CHUNK = 1024
VMEM_LIMIT_BYTES = 60 * 1024 * 1024

F32 = jnp.float32
BF16 = jnp.bfloat16


def _rms(x, g):
    var = jnp.mean(x * x, axis=-1, keepdims=True)
    return x * lax.rsqrt(var + RMS_EPS) * g


def _dot(a, b):
    return jnp.dot(a, b, preferred_element_type=F32)


def _dot_nt(a, b):
    return lax.dot_general(a, b, (((1,), (1,)), ((), ())), preferred_element_type=F32)


def _ffn_chunks(d_ff):
    return [(c, min(c + FFN_CHUNK, d_ff)) for c in range(0, d_ff, FFN_CHUNK)]


def _dense_tail(x, mixed_bf16, wout_ref, n2_ref, wg_ref, wu_ref, wd_ref, fn_ref, act_ref):
    x1 = x + _dot(mixed_bf16, wout_ref[...])
    h2 = _rms(x1, n2_ref[...]).astype(BF16)
    for c0, c1 in _ffn_chunks(wg_ref.shape[1]):
        gate = _dot(h2, wg_ref[:, c0:c1])
        up = _dot(h2, wu_ref[:, c0:c1])
        act_ref[:, c0:c1] = (jax.nn.silu(gate) * up).astype(BF16)
    x2 = x1 + _dot(act_ref[...], wd_ref[...])
    return _rms(x2, fn_ref[...])


def _split_heads_to_lane_halves(a):
    lo = lax.broadcasted_iota(jnp.int32, a.shape, 1) < HEAD_DIM
    zero = jnp.zeros_like(a)
    ar = pltpu.roll(a, HEAD_DIM, axis=1)
    return (jnp.where(lo, a, zero).astype(BF16), jnp.where(lo, zero, ar).astype(BF16),
            jnp.where(lo, ar, zero).astype(BF16), jnp.where(lo, zero, a).astype(BF16))


def _prompt_body(x_ref, n1_ref, win_ref, bias_ref, sink_ref, wpool_ref, pscale_ref, wout_ref, n2_ref,
                 wg_ref, wu_ref, wd_ref, fn_ref,
                 y_ref, kout_ref, vout_ref, pout_ref,
                 kk_ref, vv_ref, ubuf_ref, mixed_ref, act_ref, s_ref, m_ref, p_ref):
    T = x_ref.shape[1]
    t = pl.program_id(1)
    nblk = T // BLOCK
    ROWS = 2 * BLOCK
    KEYS = 2 * BLOCK

    @pl.when(t == 0)
    def _():
        kk_ref[:, 0:BLOCK, :] = jnp.zeros((4, BLOCK, LANES), BF16)
        vv_ref[:, 0:BLOCK, :] = jnp.zeros((4, BLOCK, LANES), BF16)
        ubuf_ref[0:HIST_PAD, :] = jnp.zeros((HIST_PAD, ubuf_ref.shape[1]), F32)

    x = x_ref[0]
    h = _rms(x, n1_ref[...]).astype(BF16)
    proj = _dot(h, win_ref[...])
    q = (proj[:, 0:ATTN_WIDTH] * ATTN_SCALE).astype(BF16)
    k = proj[:, ATTN_WIDTH:ATTN_WIDTH + KV_WIDTH]
    v = proj[:, ATTN_WIDTH + KV_WIDTH:ATTN_WIDTH + 2 * KV_WIDTH]
    u = proj[:, ATTN_WIDTH + 2 * KV_WIDTH:]

    kout_ref[0] = k[T - WINDOW:, :]
    vout_ref[0] = v[T - WINDOW:, :]
    ubuf_ref[HIST_PAD:, :] = u
    pout_ref[0] = ubuf_ref[HIST_PAD + T - POOL_HIST:HIST_PAD + T, :]

    for idx, part in enumerate(_split_heads_to_lane_halves(k)):
        kk_ref[idx, BLOCK:, :] = part
    for idx, part in enumerate(_split_heads_to_lane_halves(v)):
        vv_ref[idx, BLOCK:, :] = part

    first_tile = (t == 0).astype(jnp.int32)
    for j in range(nblk):
        rows = slice(j * BLOCK, (j + 1) * BLOCK)
        keys = slice(j * BLOCK, j * BLOCK + KEYS)
        for g in range(N_KV_HEADS):
            q2 = jnp.concatenate([q[rows, (2 * g) * LANES:(2 * g + 1) * LANES],
                                  q[rows, (2 * g + 1) * LANES:(2 * g + 2) * LANES]], axis=0)
            kmat = jnp.concatenate([kk_ref[2 * g, keys, :], kk_ref[2 * g + 1, keys, :]], axis=0)
            bias = bias_ref[first_tile if j == 0 else 0, g]
            s_ref[j * N_KV_HEADS + g] = _dot_nt(q2, kmat) + bias

    top = lax.broadcasted_iota(jnp.int32, (ROWS, LANES), 0) < BLOCK
    sink_b = [[jnp.where(top, sink_ref[GQA_GROUP * g + c], sink_ref[GQA_GROUP * g + 2 + c]) for c in range(2)]
              for g in range(N_KV_HEADS)]
    n_units = nblk * N_KV_HEADS
    for un in range(n_units):
        for c in range(2):
            sh = s_ref[un, :, c * KEYS:(c + 1) * KEYS]
            m_ref[un, c] = jnp.maximum(jnp.max(sh, axis=-1, keepdims=True), sink_b[un % N_KV_HEADS][c])
    for un in range(n_units):
        for c in range(2):
            sh = s_ref[un, :, c * KEYS:(c + 1) * KEYS]
            m = m_ref[un, c]
            p_ref[un, :, c * KEYS:(c + 1) * KEYS] = jnp.exp(sh - jnp.concatenate([m, m], axis=1)).astype(BF16)

    lo = lax.broadcasted_iota(jnp.int32, (ROWS, LANES), 1) < HEAD_DIM
    first_head_rows = lax.broadcasted_iota(jnp.int32, (2 * KEYS, LANES), 0) < KEYS
    first_head_cols = lax.broadcasted_iota(jnp.int32, (2 * KEYS, LANES), 1) < HEAD_DIM
    ones_cols = jnp.where(first_head_rows == first_head_cols, 1.0, 0.0).astype(BF16)
    for un in range(n_units):
        j, g = divmod(un, N_KV_HEADS)
        keys = slice(j * BLOCK, j * BLOCK + KEYS)
        vmat = jnp.concatenate([vv_ref[2 * g, keys, :], vv_ref[2 * g + 1, keys, :]], axis=0)
        o = _dot(p_ref[un], jnp.concatenate([vmat, ones_cols], axis=1))
        m_sel = jnp.where(lo, m_ref[un, 0], m_ref[un, 1])
        sink_sel = jnp.where(lo, sink_b[g][0], sink_b[g][1])
        out = o[:, 0:LANES] / (o[:, LANES:2 * LANES] + jnp.exp(sink_sel - m_sel))
        rows = slice(j * BLOCK, (j + 1) * BLOCK)
        mixed_ref[rows, (2 * g) * LANES:(2 * g + 1) * LANES] = out[0:BLOCK].astype(BF16)
        mixed_ref[rows, (2 * g + 1) * LANES:(2 * g + 2) * LANES] = out[BLOCK:ROWS].astype(BF16)

    pos = lax.broadcasted_iota(jnp.int32, (T, 1), 0) + t * T
    pooled = []
    for g, w in enumerate(POOL_WINDOWS):
        cols = slice(g * LANES, (g + 1) * LANES)
        cur = ubuf_ref[HIST_PAD:HIST_PAD + T, cols]
        win = cur
        for back in range(1, w):
            win = win + ubuf_ref[HIST_PAD - back:HIST_PAD - back + T, cols]
        count = jnp.minimum(w, pos + 1).astype(F32)
        pooled.append((win / count - cur).astype(BF16))
    for g2 in range(len(POOL_WINDOWS) // 2):
        cols = slice(2 * g2 * LANES, (2 * g2 + 2) * LANES)
        m2 = jnp.concatenate(pooled[2 * g2:2 * g2 + 2], axis=1)
        yg = _dot(m2, wpool_ref[g2]) * pscale_ref[:, cols]
        mixed_ref[:, ATTN_WIDTH + 2 * g2 * LANES:ATTN_WIDTH + (2 * g2 + 2) * LANES] = yg.astype(BF16)

    kk_ref[:, 0:BLOCK, :] = kk_ref[:, T:T + BLOCK, :]
    vv_ref[:, 0:BLOCK, :] = vv_ref[:, T:T + BLOCK, :]
    ubuf_ref[0:HIST_PAD, :] = ubuf_ref[T:T + HIST_PAD, :]

    y_ref[0] = _dense_tail(x, mixed_ref[...], wout_ref, n2_ref, wg_ref, wu_ref, wd_ref, fn_ref, act_ref)


def _sample_body(xs_ref, n1_ref, wins_ref, kc_ref, vc_ref, hist_ref, bias_ref, sinkrow_ref,
                 wpool_ref, pscale_ref, wout_ref, n2_ref, wg_ref, wu_ref, wd_ref, fn_ref,
                 ys_ref, nk_ref, nv_ref, npool_ref,
                 proj_ref, mixed_ref, act_ref):
    SB = kc_ref.shape[0]
    L = kc_ref.shape[1]
    i = pl.program_id(0)
    q_width = N_HEADS * LANES

    @pl.when(i == 0)
    def _():
        h = _rms(xs_ref[...], n1_ref[...]).astype(BF16)
        proj_ref[...] = _dot(h, wins_ref[...])

    r0 = pl.multiple_of(i * SB, SB)
    pr = proj_ref[pl.ds(r0, SB), :]
    qs = pr[:, 0:q_width] * ATTN_SCALE
    knew = pr[:, q_width:q_width + KV_WIDTH]
    vnew = pr[:, q_width + KV_WIDTH:q_width + 2 * KV_WIDTH]
    u = pr[:, q_width + 2 * KV_WIDTH:]

    nk_ref[:, 0:L - 1, :] = kc_ref[:, 1:L, :]
    nk_ref[:, L - 1, :] = knew
    nv_ref[:, 0:L - 1, :] = vc_ref[:, 1:L, :]
    nv_ref[:, L - 1, :] = vnew
    npool_ref[:, 0:POOL_HIST - 1, :] = hist_ref[:, 1:POOL_HIST, :]
    npool_ref[:, POOL_HIST - 1, :] = u

    rows64 = N_HEADS * SUBLANES
    sub = lax.broadcasted_iota(jnp.int32, (rows64, LANES), 0) % SUBLANES
    lo = lax.broadcasted_iota(jnp.int32, (SUBLANES, LANES), 1) < HEAD_DIM
    bias = bias_ref[...]
    sink = sinkrow_ref[...]
    for r in range(SB // SUBLANES):
        grp = slice(r * SUBLANES, (r + 1) * SUBLANES)
        lhs = jnp.concatenate([qs[grp, h * LANES:(h + 1) * LANES] for h in range(N_HEADS)], axis=0)
        lhs_bf = lhs.astype(BF16)
        s = jnp.zeros((rows64, L), F32)
        for si in range(SUBLANES):
            kb = kc_ref[r * SUBLANES + si].astype(BF16)
            s = jnp.where(sub == si, _dot_nt(lhs_bf, kb), s)
        s = s + bias
        knew_rep = jnp.concatenate([knew[grp]] * N_HEADS, axis=0)
        vnew_rep = jnp.concatenate([vnew[grp]] * N_HEADS, axis=0)
        s_new = jnp.sum(lhs * knew_rep, axis=-1, keepdims=True)
        m = jnp.maximum(jnp.maximum(jnp.max(s, axis=-1, keepdims=True), s_new), sink)
        e = jnp.exp(s - m)
        e_new = jnp.exp(s_new - m)
        denom = jnp.sum(e, axis=-1, keepdims=True) + e_new + jnp.exp(sink - m)
        e_bf = e.astype(BF16)
        o = jnp.zeros((rows64, LANES), F32)
        for si in range(SUBLANES):
            vb = vc_ref[r * SUBLANES + si].astype(BF16)
            o = jnp.where(sub == si, _dot(e_bf, vb), o)
        o = (o + e_new * vnew_rep) / denom
        out_rows = pl.ds(pl.multiple_of(r0 + r * SUBLANES, SUBLANES), SUBLANES)
        for p in range(N_PAIRS):
            a = o[2 * p * SUBLANES:(2 * p + 1) * SUBLANES]
            b = o[(2 * p + 1) * SUBLANES:(2 * p + 2) * SUBLANES]
            if (2 * p) // GQA_GROUP == 0:
                pair = jnp.where(lo, a, pltpu.roll(b, HEAD_DIM, axis=1))
            else:
                pair = jnp.where(lo, pltpu.roll(a, HEAD_DIM, axis=1), b)
            mixed_ref[out_rows, p * LANES:(p + 1) * LANES] = pair

    blk_rows = pl.ds(r0, SB)
    for g, w in enumerate(POOL_WINDOWS):
        cols = slice(g * LANES, (g + 1) * LANES)
        cur = u[:, cols]
        win = cur
        for back in range(1, w):
            win = win + hist_ref[:, POOL_HIST - back, cols]
        count = float(min(w, PAST_LEN + 1))
        m = (win / count - cur).astype(BF16)
        yg = _dot(m, wpool_ref[g]) * pscale_ref[:, cols]
        mixed_ref[blk_rows, ATTN_WIDTH + g * LANES:ATTN_WIDTH + (g + 1) * LANES] = yg

    @pl.when(i == pl.num_programs(0) - 1)
    def _():
        ys_ref[...] = _dense_tail(xs_ref[...], mixed_ref[...].astype(BF16), wout_ref, n2_ref,
                                  wg_ref, wu_ref, wd_ref, fn_ref, act_ref)


def _alibi_slopes():
    return np.exp2(-8.0 * np.arange(1, N_HEADS + 1, dtype=np.float32) / N_HEADS).astype(np.float32)


def _prompt_bias():
    slopes = _alibi_slopes()
    qi = np.arange(BLOCK)[:, None]
    ki = np.arange(2 * BLOCK)[None, :]
    rel = BLOCK + qi - ki
    band = (rel >= 0) & (rel <= WINDOW)
    out = np.empty((2, N_KV_HEADS, 2 * BLOCK, 4 * BLOCK), np.float32)
    for first in range(2):
        valid = band & ((ki >= BLOCK) if first else True)
        for g in range(N_KV_HEADS):
            for r in range(2):
                for c in range(2):
                    slope = slopes[GQA_GROUP * g + 2 * r + c]
                    val = np.where(valid, -slope * rel.astype(np.float32), np.float32(NEG_INF))
                    out[first, g, r * BLOCK:(r + 1) * BLOCK, c * 2 * BLOCK:(c + 1) * 2 * BLOCK] = val
    return out


def _sample_bias(L):
    assert PAST_LEN >= L and L <= WINDOW
    slopes = _alibi_slopes()
    rel = (L - np.arange(L)).astype(np.float32)
    return np.repeat(-slopes[:, None] * rel[None, :], SUBLANES, axis=0).astype(np.float32)


def _resident(shape):
    zeros = (0,) * len(shape)
    return pl.BlockSpec(shape, lambda *_: zeros, pipeline_mode=pl.Buffered(1))


def kernel(x_prompt, x_sample, cache_k_window, cache_v_window, state_pool, norm1, w_in, attn_sinks, w_pool,
           pool_scale, w_out, norm2, w_gate, w_up, w_down, final_norm):
    B, S, D = x_prompt.shape
    NS = x_sample.shape[0]
    L = cache_k_window.shape[2]
    d_ff = w_gate.shape[-1]
    pool_w = state_pool.shape[-1]
    assert norm1.shape[0] == 1 and x_sample.shape[1] == 1
    assert w_in.shape[-1] == ATTN_WIDTH + 2 * KV_WIDTH + pool_w and pool_w == len(POOL_WINDOWS) * LANES
    assert S % PROMPT_TILE == 0 and NS % SAMPLE_BLOCK == 0 and state_pool.shape[2] == POOL_HIST
    T = PROMPT_TILE

    w_in_bf = w_in[0].astype(BF16)
    w_pool_bf = w_pool[0].astype(BF16)
    w_out_bf = w_out[0].astype(BF16)
    w_gate_bf = w_gate[0].astype(BF16)
    w_up_bf = w_up[0].astype(BF16)
    w_down_bf = w_down[0].astype(BF16)
    n1 = norm1[0].reshape(1, D)
    n2 = norm2[0].reshape(1, D)
    fn = final_norm.reshape(1, D)
    pscale = pool_scale[0].reshape(1, pool_w)
    sinks = attn_sinks[0]

    mixed_w = ATTN_WIDTH + pool_w
    dense_specs = [_resident((1, pool_w)), _resident((mixed_w, D)),
                   _resident((1, D)), _resident((D, d_ff)), _resident((D, d_ff)), _resident((d_ff, D)),
                   _resident((1, D))]
    dense_args = (pscale, w_out_bf, n2, w_gate_bf, w_up_bf, w_down_bf, fn)

    zp = jnp.zeros((LANES, LANES), BF16)
    w_pool_pairs = jnp.stack([
        jnp.concatenate([jnp.concatenate([w_pool_bf[2 * i], zp], axis=1),
                         jnp.concatenate([zp, w_pool_bf[2 * i + 1]], axis=1)], axis=0)
        for i in range(len(POOL_WINDOWS) // 2)])

    n_units = (T // BLOCK) * N_KV_HEADS
    y_prompt, k_last, v_last, pool_last = pl.pallas_call(
        _prompt_body,
        grid=(B, S // T),
        in_specs=[pl.BlockSpec((1, T, D), lambda b, t: (b, t, 0)),
                  _resident((1, D)),
                  _resident(w_in_bf.shape),
                  _resident((2, N_KV_HEADS, 2 * BLOCK, 4 * BLOCK)),
                  pl.BlockSpec(memory_space=pltpu.SMEM),
                  _resident(w_pool_pairs.shape)] + dense_specs,
        out_specs=[pl.BlockSpec((1, T, D), lambda b, t: (b, t, 0)),
                   pl.BlockSpec((1, WINDOW, KV_WIDTH), lambda b, t: (b, 0, 0)),
                   pl.BlockSpec((1, WINDOW, KV_WIDTH), lambda b, t: (b, 0, 0)),
                   pl.BlockSpec((1, POOL_HIST, pool_w), lambda b, t: (b, 0, 0))],
        out_shape=[jax.ShapeDtypeStruct((B, S, D), F32),
                   jax.ShapeDtypeStruct((B, WINDOW, KV_WIDTH), F32),
                   jax.ShapeDtypeStruct((B, WINDOW, KV_WIDTH), F32),
                   jax.ShapeDtypeStruct((B, POOL_HIST, pool_w), F32)],
        scratch_shapes=[pltpu.VMEM((4, T + BLOCK, LANES), BF16),
                        pltpu.VMEM((4, T + BLOCK, LANES), BF16),
                        pltpu.VMEM((T + HIST_PAD, pool_w), F32),
                        pltpu.VMEM((T, mixed_w), BF16),
                        pltpu.VMEM((T, d_ff), BF16),
                        pltpu.VMEM((n_units, 2 * BLOCK, 4 * BLOCK), F32),
                        pltpu.VMEM((n_units, 2, 2 * BLOCK, LANES), F32),
                        pltpu.VMEM((n_units, 2 * BLOCK, 4 * BLOCK), BF16)],
        compiler_params=pltpu.CompilerParams(dimension_semantics=("arbitrary", "arbitrary"),
                                             vmem_limit_bytes=VMEM_LIMIT_BYTES),
        name="prompt_layer",
    )(x_prompt, n1, w_in_bf, jnp.asarray(_prompt_bias()), sinks, w_pool_pairs, *dense_args)

    wq = w_in[0][:, :ATTN_WIDTH].reshape(D, N_HEADS, HEAD_DIM)
    zq = jnp.zeros_like(wq)
    kv_of_head = (np.arange(N_HEADS) // GQA_GROUP)[None, :, None]
    wq_pad = jnp.concatenate([jnp.where(kv_of_head == 0, wq, zq), jnp.where(kv_of_head == 1, wq, zq)], axis=-1)
    w_in_s = jnp.concatenate([wq_pad.reshape(D, N_HEADS * LANES), w_in[0][:, ATTN_WIDTH:]], axis=-1).astype(BF16)
    in_w_s = w_in_s.shape[1]

    SB = SAMPLE_BLOCK
    sink_rows = jnp.repeat(sinks, SUBLANES).reshape(N_HEADS * SUBLANES, 1)
    y_sample, nk, nv, npool = pl.pallas_call(
        _sample_body,
        grid=(NS // SB,),
        in_specs=[_resident((NS, D)),
                  _resident((1, D)),
                  _resident((D, in_w_s)),
                  pl.BlockSpec((SB, L, KV_WIDTH), lambda i: (i, 0, 0)),
                  pl.BlockSpec((SB, L, KV_WIDTH), lambda i: (i, 0, 0)),
                  pl.BlockSpec((SB, POOL_HIST, pool_w), lambda i: (i, 0, 0)),
                  _resident((N_HEADS * SUBLANES, L)),
                  _resident((N_HEADS * SUBLANES, 1)),
                  _resident(w_pool_bf.shape)] + dense_specs,
        out_specs=[pl.BlockSpec((NS, D), lambda i: (0, 0)),
                   pl.BlockSpec((SB, L, KV_WIDTH), lambda i: (i, 0, 0)),
                   pl.BlockSpec((SB, L, KV_WIDTH), lambda i: (i, 0, 0)),
                   pl.BlockSpec((SB, POOL_HIST, pool_w), lambda i: (i, 0, 0))],
        out_shape=[jax.ShapeDtypeStruct((NS, D), F32),
                   jax.ShapeDtypeStruct((NS, L, KV_WIDTH), F32),
                   jax.ShapeDtypeStruct((NS, L, KV_WIDTH), F32),
                   jax.ShapeDtypeStruct((NS, POOL_HIST, pool_w), F32)],
        scratch_shapes=[pltpu.VMEM((NS, in_w_s), F32),
                        pltpu.VMEM((NS, mixed_w), F32),
                        pltpu.VMEM((NS, d_ff), BF16)],
        compiler_params=pltpu.CompilerParams(dimension_semantics=("arbitrary",),
                                             vmem_limit_bytes=VMEM_LIMIT_BYTES),
        name="sample_layer",
    )(x_sample.reshape(NS, D), n1, w_in_s, cache_k_window[0].reshape(NS, L, KV_WIDTH),
      cache_v_window[0].reshape(NS, L, KV_WIDTH), state_pool[0], jnp.asarray(_sample_bias(L)), sink_rows,
      w_pool_bf, *dense_args)

    kv_shape = (1, B, WINDOW, N_KV_HEADS, HEAD_DIM)
    skv_shape = (1, NS, L, N_KV_HEADS, HEAD_DIM)
    return (y_prompt, y_sample.reshape(NS, 1, D), k_last.reshape(kv_shape), v_last.reshape(kv_shape),
            pool_last[None], nk.reshape(skv_shape), nv.reshape(skv_shape), npool[None])
```

```python
import functools

import numpy as np
import jax
import jax.numpy as jnp
from jax import lax
from jax.experimental import pallas as pl
from jax.experimental.pallas import tpu as pltpu

N_HEADS = 8
N_KV_HEADS = 2
HEAD_DIM = 64
GQA_GROUP = N_HEADS // N_KV_HEADS
WINDOW = 128
BLOCK = WINDOW
PAST_LEN = 8192
POOL_WINDOWS = (2, 4, 8, 16)
POOL_HIST = max(POOL_WINDOWS) - 1
RMS_EPS = 1e-5
NEG_INF = -1e30
ATTN_SCALE = HEAD_DIM ** -0.5

ATTN_WIDTH = N_HEADS * HEAD_DIM
KV_WIDTH = N_KV_HEADS * HEAD_DIM
N_PAIRS = N_HEADS // 2
LANES = 128
SUBLANES = 8
HIST_PAD = 16

PROMPT_TILE = 512
SAMPLE_BLOCK = 32
FFN_CHUNK = 1024
VMEM_LIMIT_BYTES = 60 * 1024 * 1024

F32 = jnp.float32
BF16 = jnp.bfloat16


def _rms(x, g):
    var = jnp.mean(x * x, axis=-1, keepdims=True)
    return x * lax.rsqrt(var + RMS_EPS) * g


def _dot(a, b):
    return jnp.dot(a, b, preferred_element_type=F32)


def _dot_nt(a, b):
    return lax.dot_general(a, b, (((1,), (1,)), ((), ())), preferred_element_type=F32)


def _ffn_chunks(d_ff):
    return [(c, min(c + FFN_CHUNK, d_ff)) for c in range(0, d_ff, FFN_CHUNK)]


def _dense_tail(x, mixed_bf16, wout_ref, n2_ref, wg_ref, wu_ref, wd_ref, fn_ref, act_ref):
    x1 = x + _dot(mixed_bf16, wout_ref[...])
    h2 = _rms(x1, n2_ref[...]).astype(BF16)
    for c0, c1 in _ffn_chunks(wg_ref.shape[1]):
        gate = _dot(h2, wg_ref[:, c0:c1])
        up = _dot(h2, wu_ref[:, c0:c1])
        act_ref[:, c0:c1] = (jax.nn.silu(gate) * up).astype(BF16)
    x2 = x1 + _dot(act_ref[...], wd_ref[...])
    return _rms(x2, fn_ref[...])


def _split_heads_to_lane_halves(a):
    lo = lax.broadcasted_iota(jnp.int32, a.shape, 1) < HEAD_DIM
    zero = jnp.zeros_like(a)
    ar = pltpu.roll(a, HEAD_DIM, axis=1)
    return (jnp.where(lo, a, zero).astype(BF16), jnp.where(lo, zero, ar).astype(BF16),
            jnp.where(lo, ar, zero).astype(BF16), jnp.where(lo, zero, a).astype(BF16))


def _prompt_body(x_ref, n1_ref, win_ref, bias_ref, sink_ref, wpool_ref, pscale_ref, wout_ref, n2_ref,
                 wg_ref, wu_ref, wd_ref, fn_ref,
                 y_ref, kout_ref, vout_ref, pout_ref,
                 kk_ref, vv_ref, ubuf_ref, mixed_ref, act_ref, s_ref, m_ref, p_ref):
    T = x_ref.shape[1]
    t = pl.program_id(1)
    nblk = T // BLOCK
    ROWS = 2 * BLOCK
    KEYS = 2 * BLOCK

    @pl.when(t == 0)
    def _():
        kk_ref[:, 0:BLOCK, :] = jnp.zeros((4, BLOCK, LANES), BF16)
        vv_ref[:, 0:BLOCK, :] = jnp.zeros((4, BLOCK, LANES), BF16)
        ubuf_ref[0:HIST_PAD, :] = jnp.zeros((HIST_PAD, ubuf_ref.shape[1]), F32)

    x = x_ref[0]
    h = _rms(x, n1_ref[...]).astype(BF16)
    proj = _dot(h, win_ref[...])
    q = (proj[:, 0:ATTN_WIDTH] * ATTN_SCALE).astype(BF16)
    k = proj[:, ATTN_WIDTH:ATTN_WIDTH + KV_WIDTH]
    v = proj[:, ATTN_WIDTH + KV_WIDTH:ATTN_WIDTH + 2 * KV_WIDTH]
    u = proj[:, ATTN_WIDTH + 2 * KV_WIDTH:]

    kout_ref[0] = k[T - WINDOW:, :]
    vout_ref[0] = v[T - WINDOW:, :]
    ubuf_ref[HIST_PAD:, :] = u
    pout_ref[0] = ubuf_ref[HIST_PAD + T - POOL_HIST:HIST_PAD + T, :]

    for idx, part in enumerate(_split_heads_to_lane_halves(k)):
        kk_ref[idx, BLOCK:, :] = part
    for idx, part in enumerate(_split_heads_to_lane_halves(v)):
        vv_ref[idx, BLOCK:, :] = part

    first_tile = (t == 0).astype(jnp.int32)
    for j in range(nblk):
        rows = slice(j * BLOCK, (j + 1) * BLOCK)
        keys = slice(j * BLOCK, j * BLOCK + KEYS)
        for g in range(N_KV_HEADS):
            q2 = jnp.concatenate([q[rows, (2 * g) * LANES:(2 * g + 1) * LANES],
                                  q[rows, (2 * g + 1) * LANES:(2 * g + 2) * LANES]], axis=0)
            kmat = jnp.concatenate([kk_ref[2 * g, keys, :], kk_ref[2 * g + 1, keys, :]], axis=0)
            bias = bias_ref[first_tile if j == 0 else 0, g]
            s_ref[j * N_KV_HEADS + g] = _dot_nt(q2, kmat) + bias

    top = lax.broadcasted_iota(jnp.int32, (ROWS, LANES), 0) < BLOCK
    sink_b = [[jnp.where(top, sink_ref[GQA_GROUP * g + c], sink_ref[GQA_GROUP * g + 2 + c]) for c in range(2)]
              for g in range(N_KV_HEADS)]
    n_units = nblk * N_KV_HEADS
    for un in range(n_units):
        for c in range(2):
            sh = s_ref[un, :, c * KEYS:(c + 1) * KEYS]
            m_ref[un, c] = jnp.maximum(jnp.max(sh, axis=-1, keepdims=True), sink_b[un % N_KV_HEADS][c])
    for un in range(n_units):
        for c in range(2):
            sh = s_ref[un, :, c * KEYS:(c + 1) * KEYS]
            m = m_ref[un, c]
            p_ref[un, :, c * KEYS:(c + 1) * KEYS] = jnp.exp(sh - jnp.concatenate([m, m], axis=1)).astype(BF16)

    lo = lax.broadcasted_iota(jnp.int32, (ROWS, LANES), 1) < HEAD_DIM
    first_head_rows = lax.broadcasted_iota(jnp.int32, (2 * KEYS, LANES), 0) < KEYS
    first_head_cols = lax.broadcasted_iota(jnp.int32, (2 * KEYS, LANES), 1) < HEAD_DIM
    ones_cols = jnp.where(first_head_rows == first_head_cols, 1.0, 0.0).astype(BF16)
    for un in range(n_units):
        j, g = divmod(un, N_KV_HEADS)
        keys = slice(j * BLOCK, j * BLOCK + KEYS)
        vmat = jnp.concatenate([vv_ref[2 * g, keys, :], vv_ref[2 * g + 1, keys, :]], axis=0)
        o = _dot(p_ref[un], jnp.concatenate([vmat, ones_cols], axis=1))
        m_sel = jnp.where(lo, m_ref[un, 0], m_ref[un, 1])
        sink_sel = jnp.where(lo, sink_b[g][0], sink_b[g][1])
        out = o[:, 0:LANES] / (o[:, LANES:2 * LANES] + jnp.exp(sink_sel - m_sel))
        rows = slice(j * BLOCK, (j + 1) * BLOCK)
        mixed_ref[rows, (2 * g) * LANES:(2 * g + 1) * LANES] = out[0:BLOCK].astype(BF16)
        mixed_ref[rows, (2 * g + 1) * LANES:(2 * g + 2) * LANES] = out[BLOCK:ROWS].astype(BF16)

    pos = lax.broadcasted_iota(jnp.int32, (T, 1), 0) + t * T
    pooled = []
    for g, w in enumerate(POOL_WINDOWS):
        cols = slice(g * LANES, (g + 1) * LANES)
        cur = ubuf_ref[HIST_PAD:HIST_PAD + T, cols]
        win = cur
        for back in range(1, w):
            win = win + ubuf_ref[HIST_PAD - back:HIST_PAD - back + T, cols]
        count = jnp.minimum(w, pos + 1).astype(F32)
        pooled.append((win / count - cur).astype(BF16))
    for g2 in range(len(POOL_WINDOWS) // 2):
        cols = slice(2 * g2 * LANES, (2 * g2 + 2) * LANES)
        m2 = jnp.concatenate(pooled[2 * g2:2 * g2 + 2], axis=1)
        yg = _dot(m2, wpool_ref[g2]) * pscale_ref[:, cols]
        mixed_ref[:, ATTN_WIDTH + 2 * g2 * LANES:ATTN_WIDTH + (2 * g2 + 2) * LANES] = yg.astype(BF16)

    kk_ref[:, 0:BLOCK, :] = kk_ref[:, T:T + BLOCK, :]
    vv_ref[:, 0:BLOCK, :] = vv_ref[:, T:T + BLOCK, :]
    ubuf_ref[0:HIST_PAD, :] = ubuf_ref[T:T + HIST_PAD, :]

    y_ref[0] = _dense_tail(x, mixed_ref[...], wout_ref, n2_ref, wg_ref, wu_ref, wd_ref, fn_ref, act_ref)


def _sample_body(xs_ref, n1_ref, win_ref, kt_ref, vt_ref, hist_ref, bias_ref, sinkrow_ref,
                 wpool_ref, pscale_ref, wout_ref, n2_ref, wg_ref, wu_ref, wd_ref, fn_ref,
                 ys_ref, nkt_ref, nvt_ref, npool_ref,
                 proj_ref, mixed_ref, act_ref):
    SB = kt_ref.shape[0]
    L = kt_ref.shape[2]
    i = pl.program_id(0)

    @pl.when(i == 0)
    def _():
        h = _rms(xs_ref[...], n1_ref[...]).astype(BF16)
        proj_ref[...] = _dot(h, win_ref[...])

    r0 = pl.multiple_of(i * SB, SB)
    pr = proj_ref[pl.ds(r0, SB), :]
    q = pr[:, 0:ATTN_WIDTH] * ATTN_SCALE
    knew = pr[:, ATTN_WIDTH:ATTN_WIDTH + KV_WIDTH]
    vnew = pr[:, ATTN_WIDTH + KV_WIDTH:ATTN_WIDTH + 2 * KV_WIDTH]
    u = pr[:, ATTN_WIDTH + 2 * KV_WIDTH:]

    npool_ref[0:POOL_HIST - 1] = hist_ref[1:POOL_HIST]
    npool_ref[POOL_HIST - 1] = u

    last_lane = lax.broadcasted_iota(jnp.int32, (KV_WIDTH, L), 1) == L - 1
    pad_rows = jnp.zeros((LANES - SB, KV_WIDTH), F32)
    for new_rows, cache_ref, out_ref in ((knew, kt_ref, nkt_ref), (vnew, vt_ref, nvt_ref)):
        new_t = jnp.concatenate([new_rows, pad_rows], axis=0).T
        for s in range(SB):
            new_col = pltpu.roll(new_t, (L - 1 - s) % LANES, axis=1)
            out_ref[s] = jnp.where(last_lane, new_col, pltpu.roll(cache_ref[s], L - 1, axis=1))

    lo_blk = lax.broadcasted_iota(jnp.int32, (SB, LANES), 1) < HEAD_DIM
    zero_blk = jnp.zeros((SB, LANES), F32)
    qh = []
    for p in range(N_PAIRS):
        pair = q[:, p * LANES:(p + 1) * LANES]
        swapped = pltpu.roll(pair, HEAD_DIM, axis=1)
        if (2 * p) // GQA_GROUP == 0:
            qh += [jnp.where(lo_blk, pair, zero_blk), jnp.where(lo_blk, swapped, zero_blk)]
        else:
            qh += [jnp.where(lo_blk, zero_blk, swapped), jnp.where(lo_blk, zero_blk, pair)]

    rows64 = N_HEADS * SUBLANES
    sub = lax.broadcasted_iota(jnp.int32, (rows64, LANES), 0) % SUBLANES
    lo = lax.broadcasted_iota(jnp.int32, (SUBLANES, LANES), 1) < HEAD_DIM
    bias = bias_ref[...]
    sink = sinkrow_ref[...]
    for r in range(SB // SUBLANES):
        grp = slice(r * SUBLANES, (r + 1) * SUBLANES)
        lhs = jnp.concatenate([qh[h][grp] for h in range(N_HEADS)], axis=0)
        lhs_bf = lhs.astype(BF16)
        s = jnp.zeros((rows64, L), F32)
        for si in range(SUBLANES):
            kb = kt_ref[r * SUBLANES + si].astype(BF16)
            s = jnp.where(sub == si, _dot(lhs_bf, kb), s)
        s = s + bias
        knew_rep = jnp.concatenate([knew[grp]] * N_HEADS, axis=0)
        vnew_rep = jnp.concatenate([vnew[grp]] * N_HEADS, axis=0)
        s_new = jnp.sum(lhs * knew_rep, axis=-1, keepdims=True)
        m = jnp.maximum(jnp.maximum(jnp.max(s, axis=-1, keepdims=True), s_new), sink)
        e = jnp.exp(s - m)
        e_new = jnp.exp(s_new - m)
        denom = jnp.sum(e, axis=-1, keepdims=True) + e_new + jnp.exp(sink - m)
        e_bf = e.astype(BF16)
        o = jnp.zeros((rows64, LANES), F32)
        for si in range(SUBLANES):
            vb = vt_ref[r * SUBLANES + si].astype(BF16)
            o = jnp.where(sub == si, _dot_nt(e_bf, vb), o)
        o = (o + e_new * vnew_rep) / denom
        out_rows = pl.ds(pl.multiple_of(r0 + r * SUBLANES, SUBLANES), SUBLANES)
        for p in range(N_PAIRS):
            a = o[2 * p * SUBLANES:(2 * p + 1) * SUBLANES]
            b = o[(2 * p + 1) * SUBLANES:(2 * p + 2) * SUBLANES]
            if (2 * p) // GQA_GROUP == 0:
                pair = jnp.where(lo, a, pltpu.roll(b, HEAD_DIM, axis=1))
            else:
                pair = jnp.where(lo, pltpu.roll(a, HEAD_DIM, axis=1), b)
            mixed_ref[out_rows, p * LANES:(p + 1) * LANES] = pair

    blk_rows = pl.ds(r0, SB)
    for g, w in enumerate(POOL_WINDOWS):
        cols = slice(g * LANES, (g + 1) * LANES)
        cur = u[:, cols]
        win = cur
        for back in range(1, w):
            win = win + hist_ref[POOL_HIST - back, :, cols]
        count = float(min(w, PAST_LEN + 1))
        m = (win / count - cur).astype(BF16)
        yg = _dot(m, wpool_ref[g]) * pscale_ref[:, cols]
        mixed_ref[blk_rows, ATTN_WIDTH + g * LANES:ATTN_WIDTH + (g + 1) * LANES] = yg

    @pl.when(i == pl.num_programs(0) - 1)
    def _():
        ys_ref[...] = _dense_tail(xs_ref[...], mixed_ref[...].astype(BF16), wout_ref, n2_ref,
                                  wg_ref, wu_ref, wd_ref, fn_ref, act_ref)


def _alibi_slopes():
    return np.exp2(-8.0 * np.arange(1, N_HEADS + 1, dtype=np.float32) / N_HEADS).astype(np.float32)


def _prompt_bias():
    slopes = _alibi_slopes()
    qi = np.arange(BLOCK)[:, None]
    ki = np.arange(2 * BLOCK)[None, :]
    rel = BLOCK + qi - ki
    band = (rel >= 0) & (rel <= WINDOW)
    out = np.empty((2, N_KV_HEADS, 2 * BLOCK, 4 * BLOCK), np.float32)
    for first in range(2):
        valid = band & ((ki >= BLOCK) if first else True)
        for g in range(N_KV_HEADS):
            for r in range(2):
                for c in range(2):
                    slope = slopes[GQA_GROUP * g + 2 * r + c]
                    val = np.where(valid, -slope * rel.astype(np.float32), np.float32(NEG_INF))
                    out[first, g, r * BLOCK:(r + 1) * BLOCK, c * 2 * BLOCK:(c + 1) * 2 * BLOCK] = val
    return out


def _sample_bias(L):
    assert PAST_LEN >= L and L <= WINDOW
    slopes = _alibi_slopes()
    rel = (L - np.arange(L)).astype(np.float32)
    return np.repeat(-slopes[:, None] * rel[None, :], SUBLANES, axis=0).astype(np.float32)


def _resident(shape):
    zeros = (0,) * len(shape)
    return pl.BlockSpec(shape, lambda *_: zeros, pipeline_mode=pl.Buffered(1))


def kernel(x_prompt, x_sample, cache_k_window, cache_v_window, state_pool, norm1, w_in, attn_sinks, w_pool,
           pool_scale, w_out, norm2, w_gate, w_up, w_down, final_norm):
    B, S, D = x_prompt.shape
    NS = x_sample.shape[0]
    L = cache_k_window.shape[2]
    d_ff = w_gate.shape[-1]
    pool_w = state_pool.shape[-1]
    assert norm1.shape[0] == 1 and x_sample.shape[1] == 1
    assert w_in.shape[-1] == ATTN_WIDTH + 2 * KV_WIDTH + pool_w and pool_w == len(POOL_WINDOWS) * LANES
    assert S % PROMPT_TILE == 0 and NS % SAMPLE_BLOCK == 0 and state_pool.shape[2] == POOL_HIST
    T = PROMPT_TILE

    w_in_bf = w_in[0].astype(BF16)
    w_pool_bf = w_pool[0].astype(BF16)
    w_out_bf = w_out[0].astype(BF16)
    w_gate_bf = w_gate[0].astype(BF16)
    w_up_bf = w_up[0].astype(BF16)
    w_down_bf = w_down[0].astype(BF16)
    n1 = norm1[0].reshape(1, D)
    n2 = norm2[0].reshape(1, D)
    fn = final_norm.reshape(1, D)
    pscale = pool_scale[0].reshape(1, pool_w)
    sinks = attn_sinks[0]

    mixed_w = ATTN_WIDTH + pool_w
    dense_specs = [_resident((1, pool_w)), _resident((mixed_w, D)),
                   _resident((1, D)), _resident((D, d_ff)), _resident((D, d_ff)), _resident((d_ff, D)),
                   _resident((1, D))]
    dense_args = (pscale, w_out_bf, n2, w_gate_bf, w_up_bf, w_down_bf, fn)

    zp = jnp.zeros((LANES, LANES), BF16)
    w_pool_pairs = jnp.stack([
        jnp.concatenate([jnp.concatenate([w_pool_bf[2 * i], zp], axis=1),
                         jnp.concatenate([zp, w_pool_bf[2 * i + 1]], axis=1)], axis=0)
        for i in range(len(POOL_WINDOWS) // 2)])

    n_units = (T // BLOCK) * N_KV_HEADS
    y_prompt, k_last, v_last, pool_last = pl.pallas_call(
        _prompt_body,
        grid=(B, S // T),
        in_specs=[pl.BlockSpec((1, T, D), lambda b, t: (b, t, 0)),
                  _resident((1, D)),
                  _resident(w_in_bf.shape),
                  _resident((2, N_KV_HEADS, 2 * BLOCK, 4 * BLOCK)),
                  pl.BlockSpec(memory_space=pltpu.SMEM),
                  _resident(w_pool_pairs.shape)] + dense_specs,
        out_specs=[pl.BlockSpec((1, T, D), lambda b, t: (b, t, 0)),
                   pl.BlockSpec((1, WINDOW, KV_WIDTH), lambda b, t: (b, 0, 0)),
                   pl.BlockSpec((1, WINDOW, KV_WIDTH), lambda b, t: (b, 0, 0)),
                   pl.BlockSpec((1, POOL_HIST, pool_w), lambda b, t: (b, 0, 0))],
        out_shape=[jax.ShapeDtypeStruct((B, S, D), F32),
                   jax.ShapeDtypeStruct((B, WINDOW, KV_WIDTH), F32),
                   jax.ShapeDtypeStruct((B, WINDOW, KV_WIDTH), F32),
                   jax.ShapeDtypeStruct((B, POOL_HIST, pool_w), F32)],
        scratch_shapes=[pltpu.VMEM((4, T + BLOCK, LANES), BF16),
                        pltpu.VMEM((4, T + BLOCK, LANES), BF16),
                        pltpu.VMEM((T + HIST_PAD, pool_w), F32),
                        pltpu.VMEM((T, mixed_w), BF16),
                        pltpu.VMEM((T, d_ff), BF16),
                        pltpu.VMEM((n_units, 2 * BLOCK, 4 * BLOCK), F32),
                        pltpu.VMEM((n_units, 2, 2 * BLOCK, LANES), F32),
                        pltpu.VMEM((n_units, 2 * BLOCK, 4 * BLOCK), BF16)],
        compiler_params=pltpu.CompilerParams(dimension_semantics=("arbitrary", "arbitrary"),
                                             vmem_limit_bytes=VMEM_LIMIT_BYTES),
        name="prompt_layer",
    )(x_prompt, n1, w_in_bf, jnp.asarray(_prompt_bias()), sinks, w_pool_pairs, *dense_args)

    def cache_to_kernel(c):
        return jnp.transpose(c[0], (0, 2, 3, 1)).reshape(NS, KV_WIDTH, L)

    def cache_from_kernel(c):
        return jnp.transpose(c.reshape(NS, N_KV_HEADS, HEAD_DIM, L), (0, 3, 1, 2))[None]

    SB = SAMPLE_BLOCK
    sink_rows = jnp.repeat(sinks, SUBLANES).reshape(N_HEADS * SUBLANES, 1)
    y_sample, nkt, nvt, npool = pl.pallas_call(
        _sample_body,
        grid=(NS // SB,),
        in_specs=[_resident((NS, D)),
                  _resident((1, D)),
                  _resident(w_in_bf.shape),
                  pl.BlockSpec((SB, KV_WIDTH, L), lambda i: (i, 0, 0)),
                  pl.BlockSpec((SB, KV_WIDTH, L), lambda i: (i, 0, 0)),
                  pl.BlockSpec((POOL_HIST, SB, pool_w), lambda i: (0, i, 0)),
                  _resident((N_HEADS * SUBLANES, L)),
                  _resident((N_HEADS * SUBLANES, 1)),
                  _resident(w_pool_bf.shape)] + dense_specs,
        out_specs=[pl.BlockSpec((NS, D), lambda i: (0, 0)),
                   pl.BlockSpec((SB, KV_WIDTH, L), lambda i: (i, 0, 0)),
                   pl.BlockSpec((SB, KV_WIDTH, L), lambda i: (i, 0, 0)),
                   pl.BlockSpec((POOL_HIST, SB, pool_w), lambda i: (0, i, 0))],
        out_shape=[jax.ShapeDtypeStruct((NS, D), F32),
                   jax.ShapeDtypeStruct((NS, KV_WIDTH, L), F32),
                   jax.ShapeDtypeStruct((NS, KV_WIDTH, L), F32),
                   jax.ShapeDtypeStruct((POOL_HIST, NS, pool_w), F32)],
        scratch_shapes=[pltpu.VMEM((NS, w_in_bf.shape[1]), F32),
                        pltpu.VMEM((NS, mixed_w), F32),
                        pltpu.VMEM((NS, d_ff), BF16)],
        compiler_params=pltpu.CompilerParams(dimension_semantics=("arbitrary",),
                                             vmem_limit_bytes=VMEM_LIMIT_BYTES),
        name="sample_layer",
    )(x_sample.reshape(NS, D), n1, w_in_bf, cache_to_kernel(cache_k_window), cache_to_kernel(cache_v_window),
      jnp.transpose(state_pool[0], (1, 0, 2)), jnp.asarray(_sample_bias(L)), sink_rows, w_pool_bf, *dense_args)

    kv_shape = (1, B, WINDOW, N_KV_HEADS, HEAD_DIM)
    return (y_prompt, y_sample.reshape(NS, 1, D), k_last.reshape(kv_shape), v_last.reshape(kv_shape),
            pool_last[None], cache_from_kernel(nkt), cache_from_kernel(nvt),
            jnp.transpose(npool, (1, 0, 2))[None])
```

```python
import functools

import numpy as np
import jax
import jax.numpy as jnp
from jax import lax
from jax.experimental import pallas as pl
from jax.experimental.pallas import tpu as pltpu

N_HEADS = 8
N_KV_HEADS = 2
HEAD_DIM = 64
GQA_GROUP = N_HEADS // N_KV_HEADS
WINDOW = 128
BLOCK = WINDOW
PAST_LEN = 8192
POOL_WINDOWS = (2, 4, 8, 16)
POOL_HIST = max(POOL_WINDOWS) - 1
RMS_EPS = 1e-5
NEG_INF = -1e30
ATTN_SCALE = HEAD_DIM ** -0.5

ATTN_WIDTH = N_HEADS * HEAD_DIM
KV_WIDTH = N_KV_HEADS * HEAD_DIM
N_PAIRS = N_HEADS // 2
LANES = 128
SUBLANES = 8
MXU_DEPTH = 256
HIST_PAD = 16

PROMPT_TILE = 512
SAMPLE_BLOCK = 32
FFN_CHUNK = 512
VMEM_LIMIT_BYTES = 60 * 1024 * 1024

F32 = jnp.float32
BF16 = jnp.bfloat16


def _rms(x, g):
    var = jnp.mean(x * x, axis=-1, keepdims=True)
    return x * lax.rsqrt(var + RMS_EPS) * g


def _dot(a, b):
    return jnp.dot(a, b, preferred_element_type=F32)


def _dot_nt(a, b):
    return lax.dot_general(a, b, (((1,), (1,)), ((), ())), preferred_element_type=F32)


def _ffn_chunks(d_ff):
    return [(c, min(c + FFN_CHUNK, d_ff)) for c in range(0, d_ff, FFN_CHUNK)]


def _dense_open(x, mixed_bf16, wout_ref, n2_ref):
    x1 = x + _dot(mixed_bf16, wout_ref[...])
    return x1, _rms(x1, n2_ref[...]).astype(BF16)


def _ffn_chunk(h2, cols, wg_ref, wu_ref, act_ref):
    gate = _dot(h2, wg_ref[:, cols])
    up = _dot(h2, wu_ref[:, cols])
    act_ref[:, cols] = (jax.nn.silu(gate) * up).astype(BF16)


def _dense_close(x1, act_bf16, wd_ref, fn_ref):
    return _rms(x1 + _dot(act_bf16, wd_ref[...]), fn_ref[...])


def _after(value, anchors):
    acc = anchors[0]
    for a in anchors[1:]:
        acc = acc + a
    bits = lax.bitcast_convert_type(acc, jnp.int32)
    zero = lax.shift_right_logical(lax.shift_right_logical(bits, 16), 16)[0:1, 0:1]
    head = value[:, 0:MXU_DEPTH] + zero.astype(value.dtype)
    return jnp.concatenate([head, value[:, MXU_DEPTH:]], axis=1)


def _split_heads_to_lane_halves(a):
    lo = lax.broadcasted_iota(jnp.int32, a.shape, 1) < HEAD_DIM
    zero = jnp.zeros_like(a)
    ar = pltpu.roll(a, HEAD_DIM, axis=1)
    return (jnp.where(lo, a, zero).astype(BF16), jnp.where(lo, zero, ar).astype(BF16),
            jnp.where(lo, ar, zero).astype(BF16), jnp.where(lo, zero, a).astype(BF16))


def _load_weights_as_bf16(weights, ring_ref, sem):
    n_slots, rows, max_cols = ring_ref.shape
    chunks = [(src, dst, r0, c0, min(max_cols, dst.shape[1] - c0))
              for src, dst in weights
              for r0 in range(0, dst.shape[0], rows)
              for c0 in range(0, dst.shape[1], max_cols)]

    def chunk_copy(i):
        src, _, r0, c0, cols = chunks[i]
        slot = i % n_slots
        return pltpu.make_async_copy(src.at[0, pl.ds(r0, rows), pl.ds(c0, cols)],
                                     ring_ref.at[slot, :, 0:cols], sem.at[slot])

    for i in range(min(n_slots - 1, len(chunks))):
        chunk_copy(i).start()
    for i, (_, dst, r0, c0, cols) in enumerate(chunks):
        nxt = i + n_slots - 1
        if nxt < len(chunks):
            chunk_copy(nxt).start()
        chunk_copy(i).wait()
        dst[r0:r0 + rows, c0:c0 + cols] = ring_ref[i % n_slots, :, 0:cols].astype(BF16)


def _prompt_body(x_ref, xd_ref, xs_ref, mixs_ref, n1_ref, bias_ref, sink_ref, pscale_ref, n2_ref, fn_ref,
                 kc_ref, vc_ref, knewt_ref, vnewt_ref,
                 win_hbm, wpool_hbm, wout_hbm, wg_hbm, wu_hbm, wd_hbm,
                 y_ref, kout_ref, vout_ref, pout_ref, ys_ref, nkc_ref, nvc_ref,
                 win_ref, wpool_ref, wout_ref, wg_ref, wu_ref, wd_ref, pool_stage_ref, dma_sem,
                 kt_ref, kprev_ref, vv_ref, ubuf_ref, mixed_ref, act_ref, s_ref, m_ref, p_ref,
                 *, tiles_per_seq, n_tiles):
    T = x_ref.shape[1]
    NS = xs_ref.shape[0]
    step = pl.program_id(0)
    t = jnp.minimum(step, n_tiles - 1) % tiles_per_seq
    nblk = T // BLOCK
    ROWS = 2 * BLOCK
    KEYS = 2 * BLOCK

    @pl.when(t == 0)
    def _():
        kprev_ref[...] = jnp.zeros(kprev_ref.shape, BF16)
        vv_ref[:, 0:BLOCK, :] = jnp.zeros((4, BLOCK, LANES), BF16)
        ubuf_ref[0:HIST_PAD, :] = jnp.zeros((HIST_PAD, ubuf_ref.shape[1]), F32)

    def update_sample_caches():
        CS, _, L = kc_ref.shape
        last_lane = lax.broadcasted_iota(jnp.int32, (KV_WIDTH, L), 1) == L - 1
        first_sample = jnp.minimum(step, n_tiles - 1) * CS
        for new_t_ref, cache_ref, out_ref in ((knewt_ref, kc_ref, nkc_ref), (vnewt_ref, vc_ref, nvc_ref)):
            new_t = new_t_ref[...]
            for i in range(CS):
                new_col = pltpu.roll(new_t, L - 1 - (first_sample + i), axis=1)
                out_ref[i] = jnp.where(last_lane, new_col, pltpu.roll(cache_ref[i], L - 1, axis=1))

    def mixer():
        return _mixer_phases(x_ref, n1_ref, win_ref, bias_ref, sink_ref, wpool_ref, pscale_ref,
                             kout_ref, vout_ref, pout_ref, kt_ref, kprev_ref, vv_ref, ubuf_ref, mixed_ref,
                             s_ref, m_ref, p_ref, t)

    chunks = [slice(c0, c1) for c0, c1 in _ffn_chunks(wg_ref.shape[1])]

    @pl.when(step == 0)
    def _():
        pool_copy = pltpu.make_async_copy(wpool_hbm.at[0], pool_stage_ref, dma_sem.at[s_ref.shape[0]])
        pool_copy.start()
        _load_weights_as_bf16([(win_hbm, win_ref), (wout_hbm, wout_ref), (wg_hbm, wg_ref), (wu_hbm, wu_ref),
                               (wd_hbm, wd_ref)], s_ref, dma_sem)
        pool_copy.wait()
        zp = jnp.zeros((LANES, LANES), BF16)
        for i in range(wpool_ref.shape[0]):
            wa = pool_stage_ref[2 * i].astype(BF16)
            wb = pool_stage_ref[2 * i + 1].astype(BF16)
            wpool_ref[i] = jnp.concatenate([jnp.concatenate([wa, zp], axis=1),
                                            jnp.concatenate([zp, wb], axis=1)], axis=0)
        update_sample_caches()
        sample_act = act_ref.at[0:NS]
        x1, h2 = _dense_open(xs_ref[:, 0, :], mixs_ref[...], wout_ref, n2_ref)
        for cols in chunks:
            _ffn_chunk(h2, cols, wg_ref, wu_ref, sample_act)
        ys_ref[:, 0, :] = _dense_close(x1, sample_act[...], wd_ref, fn_ref)
        for _ in mixer():
            pass

    @pl.when(step == n_tiles)
    def _():
        x1, h2 = _dense_open(xd_ref[0], mixed_ref[...], wout_ref, n2_ref)
        for cols in chunks:
            _ffn_chunk(h2, cols, wg_ref, wu_ref, act_ref)
        y_ref[0] = _dense_close(x1, act_ref[...], wd_ref, fn_ref)

    @pl.when(jnp.logical_and(step > 0, step < n_tiles))
    def _():
        update_sample_caches()
        x1, h2 = _dense_open(xd_ref[0], mixed_ref[...], wout_ref, n2_ref)
        assert len(chunks) == 6
        phases = mixer()
        proj_done = next(phases)
        _ffn_chunk(h2, chunks[0], wg_ref, wu_ref, act_ref)
        _ffn_chunk(h2, chunks[1], wg_ref, wu_ref, act_ref)
        _ffn_chunk(_after(h2, proj_done), chunks[2], wg_ref, wu_ref, act_ref)
        score_done = next(phases)
        _ffn_chunk(h2, chunks[3], wg_ref, wu_ref, act_ref)
        _ffn_chunk(_after(h2, score_done), chunks[4], wg_ref, wu_ref, act_ref)
        exp_done = next(phases)
        _ffn_chunk(_after(h2, exp_done), chunks[5], wg_ref, wu_ref, act_ref)
        next(phases)
        y_ref[0] = _dense_close(x1, act_ref[...], wd_ref, fn_ref)


def _mixer_phases(x_ref, n1_ref, win_ref, bias_ref, sink_ref, wpool_ref, pscale_ref,
                  kout_ref, vout_ref, pout_ref, kt_ref, kprev_ref, vv_ref, ubuf_ref, mixed_ref,
                  s_ref, m_ref, p_ref, t):
    T = x_ref.shape[1]
    nblk = T // BLOCK
    ROWS = 2 * BLOCK
    KEYS = 2 * BLOCK

    x = x_ref[0]
    h = _rms(x, n1_ref[...]).astype(BF16)
    proj = _dot(h, win_ref[...])
    yield [proj[T - SUBLANES:, proj.shape[1] - LANES:]]
    q = (proj[:, 0:ATTN_WIDTH] * ATTN_SCALE).astype(BF16)
    k = proj[:, ATTN_WIDTH:ATTN_WIDTH + KV_WIDTH]
    v = proj[:, ATTN_WIDTH + KV_WIDTH:ATTN_WIDTH + 2 * KV_WIDTH]
    u = proj[:, ATTN_WIDTH + 2 * KV_WIDTH:]

    ubuf_ref[HIST_PAD:, :] = u
    pout_ref[0] = ubuf_ref[HIST_PAD + T - POOL_HIST:HIST_PAD + T, :]

    k_t = k.T
    kt_ref[...] = k_t.astype(BF16)
    kout_ref[0] = k_t[:, T - WINDOW:]
    vout_ref[0] = v[T - WINDOW:, :].T
    for idx, part in enumerate(_split_heads_to_lane_halves(v)):
        vv_ref[idx, BLOCK:, :] = part

    first_tile = (t == 0).astype(jnp.int32)
    score_done = []
    no_keys = jnp.zeros((HEAD_DIM, KEYS), BF16)
    for j in range(nblk):
        rows = slice(j * BLOCK, (j + 1) * BLOCK)
        for g in range(N_KV_HEADS):
            q2 = jnp.concatenate([q[rows, (2 * g) * LANES:(2 * g + 1) * LANES],
                                  q[rows, (2 * g + 1) * LANES:(2 * g + 2) * LANES]], axis=0)
            dims = slice(g * HEAD_DIM, (g + 1) * HEAD_DIM)
            if j == 0:
                kg = jnp.concatenate([kprev_ref[dims, :], kt_ref[dims, 0:BLOCK]], axis=1)
            else:
                kg = kt_ref[dims, (j - 1) * BLOCK:(j + 1) * BLOCK]
            kmat = jnp.concatenate([jnp.concatenate([kg, no_keys], axis=1),
                                    jnp.concatenate([no_keys, kg], axis=1)], axis=0)
            bias = bias_ref[first_tile if j == 0 else 0, g]
            sc = _dot(q2, kmat) + bias
            s_ref[j * N_KV_HEADS + g] = sc
            score_done.append(sc[ROWS - SUBLANES:, 2 * KEYS - LANES:])
    yield score_done

    top = lax.broadcasted_iota(jnp.int32, (ROWS, LANES), 0) < BLOCK
    sink_b = [[jnp.where(top, sink_ref[GQA_GROUP * g + c], sink_ref[GQA_GROUP * g + 2 + c]) for c in range(2)]
              for g in range(N_KV_HEADS)]
    n_units = nblk * N_KV_HEADS
    for un in range(n_units):
        for c in range(2):
            sh = s_ref[un, :, c * KEYS:(c + 1) * KEYS]
            m_ref[un, c] = jnp.maximum(jnp.max(sh, axis=-1, keepdims=True), sink_b[un % N_KV_HEADS][c])
    exp_done = []
    for un in range(n_units):
        for c in range(2):
            sh = s_ref[un, :, c * KEYS:(c + 1) * KEYS]
            m = m_ref[un, c]
            e = jnp.exp(sh - jnp.concatenate([m, m], axis=1))
            p_ref[un, :, c * KEYS:(c + 1) * KEYS] = e.astype(BF16)
            exp_done.append(e[ROWS - SUBLANES:, KEYS - LANES:])
    yield exp_done

    lo = lax.broadcasted_iota(jnp.int32, (ROWS, LANES), 1) < HEAD_DIM
    first_head_rows = lax.broadcasted_iota(jnp.int32, (2 * KEYS, LANES), 0) < KEYS
    first_head_cols = lax.broadcasted_iota(jnp.int32, (2 * KEYS, LANES), 1) < HEAD_DIM
    ones_cols = jnp.where(first_head_rows == first_head_cols, 1.0, 0.0).astype(BF16)
    for un in range(n_units):
        j, g = divmod(un, N_KV_HEADS)
        keys = slice(j * BLOCK, j * BLOCK + KEYS)
        vmat = jnp.concatenate([vv_ref[2 * g, keys, :], vv_ref[2 * g + 1, keys, :]], axis=0)
        o = _dot(p_ref[un], jnp.concatenate([vmat, ones_cols], axis=1))
        m_sel = jnp.where(lo, m_ref[un, 0], m_ref[un, 1])
        sink_sel = jnp.where(lo, sink_b[g][0], sink_b[g][1])
        out = o[:, 0:LANES] / (o[:, LANES:2 * LANES] + jnp.exp(sink_sel - m_sel))
        rows = slice(j * BLOCK, (j + 1) * BLOCK)
        mixed_ref[rows, (2 * g) * LANES:(2 * g + 1) * LANES] = out[0:BLOCK].astype(BF16)
        mixed_ref[rows, (2 * g + 1) * LANES:(2 * g + 2) * LANES] = out[BLOCK:ROWS].astype(BF16)

    pos = lax.broadcasted_iota(jnp.int32, (T, 1), 0) + t * T
    pooled = []
    for g, w in enumerate(POOL_WINDOWS):
        cols = slice(g * LANES, (g + 1) * LANES)
        cur = ubuf_ref[HIST_PAD:HIST_PAD + T, cols]
        win = cur
        for back in range(1, w):
            win = win + ubuf_ref[HIST_PAD - back:HIST_PAD - back + T, cols]
        count = jnp.minimum(w, pos + 1).astype(F32)
        pooled.append((win / count - cur).astype(BF16))
    for g2 in range(len(POOL_WINDOWS) // 2):
        cols = slice(2 * g2 * LANES, (2 * g2 + 2) * LANES)
        m2 = jnp.concatenate(pooled[2 * g2:2 * g2 + 2], axis=1)
        yg = _dot(m2, wpool_ref[g2]) * pscale_ref[:, cols]
        mixed_ref[:, ATTN_WIDTH + 2 * g2 * LANES:ATTN_WIDTH + (2 * g2 + 2) * LANES] = yg.astype(BF16)

    kprev_ref[...] = kt_ref[:, T - BLOCK:T]
    vv_ref[:, 0:BLOCK, :] = vv_ref[:, T:T + BLOCK, :]
    ubuf_ref[0:HIST_PAD, :] = ubuf_ref[T:T + HIST_PAD, :]
    yield None


def _sample_body(xs_ref, n1_ref, win_ref, kt_ref, vt_ref, hist_ref, bias_ref, sinkrow_ref, wpool_ref, pscale_ref,
                 mix_ref, knewt_ref, vnewt_ref, npool_ref,
                 proj_ref, mixed_ref):
    SB = kt_ref.shape[0]
    L = kt_ref.shape[2]
    i = pl.program_id(0)

    @pl.when(i == 0)
    def _():
        h = _rms(xs_ref[:, 0, :], n1_ref[...]).astype(BF16)
        proj = _dot(h, win_ref[0].astype(BF16))
        proj_ref[...] = proj
        knewt_ref[...] = proj[:, ATTN_WIDTH:ATTN_WIDTH + KV_WIDTH].T
        vnewt_ref[...] = proj[:, ATTN_WIDTH + KV_WIDTH:ATTN_WIDTH + 2 * KV_WIDTH].T

    r0 = pl.multiple_of(i * SB, SB)
    pr = proj_ref[pl.ds(r0, SB), :]
    q = pr[:, 0:ATTN_WIDTH] * ATTN_SCALE
    knew = pr[:, ATTN_WIDTH:ATTN_WIDTH + KV_WIDTH]
    vnew = pr[:, ATTN_WIDTH + KV_WIDTH:ATTN_WIDTH + 2 * KV_WIDTH]
    u = pr[:, ATTN_WIDTH + 2 * KV_WIDTH:]

    npool_ref[0:POOL_HIST - 1] = hist_ref[1:POOL_HIST]
    npool_ref[POOL_HIST - 1] = u

    lo_blk = lax.broadcasted_iota(jnp.int32, (SB, LANES), 1) < HEAD_DIM
    zero_blk = jnp.zeros((SB, LANES), F32)
    qh = []
    for p in range(N_PAIRS):
        pair = q[:, p * LANES:(p + 1) * LANES]
        swapped = pltpu.roll(pair, HEAD_DIM, axis=1)
        if (2 * p) // GQA_GROUP == 0:
            qh += [jnp.where(lo_blk, pair, zero_blk), jnp.where(lo_blk, swapped, zero_blk)]
        else:
            qh += [jnp.where(lo_blk, zero_blk, swapped), jnp.where(lo_blk, zero_blk, pair)]

    rows64 = N_HEADS * SUBLANES
    sub = lax.broadcasted_iota(jnp.int32, (rows64, LANES), 0) % SUBLANES
    lo = lax.broadcasted_iota(jnp.int32, (SUBLANES, LANES), 1) < HEAD_DIM
    n_groups = SB // SUBLANES
    groups = [slice(r * SUBLANES, (r + 1) * SUBLANES) for r in range(n_groups)]
    lhs = jnp.concatenate([qh[h][grp] for grp in groups for h in range(N_HEADS)], axis=0)
    lhs_bf = lhs.astype(BF16)
    scores = []
    for r in range(n_groups):
        s = jnp.zeros((rows64, L), F32)
        for si in range(SUBLANES):
            kb = kt_ref[r * SUBLANES + si].astype(BF16)
            s = jnp.where(sub == si, _dot(lhs_bf[r * rows64:(r + 1) * rows64], kb), s)
        scores.append(s)
    s = jnp.concatenate(scores, axis=0) + jnp.concatenate([bias_ref[...]] * n_groups, axis=0)
    sink = jnp.concatenate([sinkrow_ref[...]] * n_groups, axis=0)
    knew_rep = jnp.concatenate([knew[grp] for grp in groups for _ in range(N_HEADS)], axis=0)
    vnew_rep = jnp.concatenate([vnew[grp] for grp in groups for _ in range(N_HEADS)], axis=0)
    s_new = jnp.sum(lhs * knew_rep, axis=-1, keepdims=True)
    m = jnp.maximum(jnp.maximum(jnp.max(s, axis=-1, keepdims=True), s_new), sink)
    e = jnp.exp(s - m)
    e_new = jnp.exp(s_new - m)
    denom = jnp.sum(e, axis=-1, keepdims=True) + e_new + jnp.exp(sink - m)
    e_bf = e.astype(BF16)
    outs = []
    for r in range(n_groups):
        o = jnp.zeros((rows64, LANES), F32)
        for si in range(SUBLANES):
            vb = vt_ref[r * SUBLANES + si].astype(BF16)
            o = jnp.where(sub == si, _dot_nt(e_bf[r * rows64:(r + 1) * rows64], vb), o)
        outs.append(o)
    o_all = (jnp.concatenate(outs, axis=0) + e_new * vnew_rep) / denom
    for r in range(n_groups):
        o = o_all[r * rows64:(r + 1) * rows64]
        out_rows = pl.ds(pl.multiple_of(r0 + r * SUBLANES, SUBLANES), SUBLANES)
        for p in range(N_PAIRS):
            a = o[2 * p * SUBLANES:(2 * p + 1) * SUBLANES]
            b = o[(2 * p + 1) * SUBLANES:(2 * p + 2) * SUBLANES]
            if (2 * p) // GQA_GROUP == 0:
                pair = jnp.where(lo, a, pltpu.roll(b, HEAD_DIM, axis=1))
            else:
                pair = jnp.where(lo, pltpu.roll(a, HEAD_DIM, axis=1), b)
            mixed_ref[out_rows, p * LANES:(p + 1) * LANES] = pair

    blk_rows = pl.ds(r0, SB)
    for g, w in enumerate(POOL_WINDOWS):
        cols = slice(g * LANES, (g + 1) * LANES)
        cur = u[:, cols]
        win = cur
        for back in range(1, w):
            win = win + hist_ref[POOL_HIST - back, :, cols]
        count = float(min(w, PAST_LEN + 1))
        m = (win / count - cur).astype(BF16)
        yg = _dot(m, wpool_ref[0, g].astype(BF16)) * pscale_ref[:, cols]
        mixed_ref[blk_rows, ATTN_WIDTH + g * LANES:ATTN_WIDTH + (g + 1) * LANES] = yg

    @pl.when(i == pl.num_programs(0) - 1)
    def _():
        mix_ref[...] = mixed_ref[...].astype(BF16)


def _alibi_slopes():
    return np.exp2(-8.0 * np.arange(1, N_HEADS + 1, dtype=np.float32) / N_HEADS).astype(np.float32)


def _prompt_bias():
    slopes = _alibi_slopes()
    qi = np.arange(BLOCK)[:, None]
    ki = np.arange(2 * BLOCK)[None, :]
    rel = BLOCK + qi - ki
    band = (rel >= 0) & (rel <= WINDOW)
    out = np.empty((2, N_KV_HEADS, 2 * BLOCK, 4 * BLOCK), np.float32)
    for first in range(2):
        valid = band & ((ki >= BLOCK) if first else True)
        for g in range(N_KV_HEADS):
            for r in range(2):
                for c in range(2):
                    slope = slopes[GQA_GROUP * g + 2 * r + c]
                    val = np.where(valid, -slope * rel.astype(np.float32), np.float32(NEG_INF))
                    out[first, g, r * BLOCK:(r + 1) * BLOCK, c * 2 * BLOCK:(c + 1) * 2 * BLOCK] = val
    return out


def _sample_bias(L):
    assert PAST_LEN >= L and L <= WINDOW
    slopes = _alibi_slopes()
    rel = (L - np.arange(L)).astype(np.float32)
    return np.repeat(-slopes[:, None] * rel[None, :], SUBLANES, axis=0).astype(np.float32)


def _resident(shape):
    zeros = (0,) * len(shape)
    return pl.BlockSpec(shape, lambda *_: zeros, pipeline_mode=pl.Buffered(1))


def kernel(x_prompt, x_sample, cache_k_window, cache_v_window, state_pool, norm1, w_in, attn_sinks, w_pool,
           pool_scale, w_out, norm2, w_gate, w_up, w_down, final_norm):
    B, S, D = x_prompt.shape
    NS = x_sample.shape[0]
    L = cache_k_window.shape[2]
    d_ff = w_gate.shape[-1]
    pool_w = state_pool.shape[-1]
    assert norm1.shape[0] == 1 and x_sample.shape[1] == 1
    assert w_in.shape[-1] == ATTN_WIDTH + 2 * KV_WIDTH + pool_w and pool_w == len(POOL_WINDOWS) * LANES
    assert S % PROMPT_TILE == 0 and NS % SAMPLE_BLOCK == 0 and state_pool.shape[2] == POOL_HIST
    T = PROMPT_TILE

    assert D % (2 * BLOCK) == 0 and d_ff % (2 * BLOCK) == 0 and NS <= T
    in_w = w_in.shape[-1]
    n1 = norm1[0].reshape(1, D)
    n2 = norm2[0].reshape(1, D)
    fn = final_norm.reshape(1, D)
    pscale = pool_scale[0].reshape(1, pool_w)
    sinks = attn_sinks[0]
    xs = x_sample
    mixed_w = ATTN_WIDTH + pool_w

    def cache_to_kernel(c):
        return jnp.transpose(c[0], (0, 2, 3, 1)).reshape(NS, KV_WIDTH, L)

    def cache_from_kernel(c):
        n, _, keys = c.shape
        return jnp.transpose(c.reshape(n, N_KV_HEADS, HEAD_DIM, keys), (0, 3, 1, 2))[None]

    SB = SAMPLE_BLOCK
    kt_cache = cache_to_kernel(cache_k_window)
    vt_cache = cache_to_kernel(cache_v_window)
    sink_rows = jnp.repeat(sinks, SUBLANES).reshape(N_HEADS * SUBLANES, 1)
    mix_sample, knew_t, vnew_t, npool = pl.pallas_call(
        _sample_body,
        grid=(NS // SB,),
        in_specs=[_resident((NS, 1, D)),
                  _resident((1, D)),
                  _resident(w_in.shape),
                  pl.BlockSpec((SB, KV_WIDTH, L), lambda i: (i, 0, 0)),
                  pl.BlockSpec((SB, KV_WIDTH, L), lambda i: (i, 0, 0)),
                  pl.BlockSpec((POOL_HIST, SB, pool_w), lambda i: (0, i, 0)),
                  _resident((N_HEADS * SUBLANES, L)),
                  _resident((N_HEADS * SUBLANES, 1)),
                  _resident(w_pool.shape),
                  _resident((1, pool_w))],
        out_specs=[pl.BlockSpec((NS, mixed_w), lambda i: (0, 0)),
                   pl.BlockSpec((KV_WIDTH, NS), lambda i: (0, 0)),
                   pl.BlockSpec((KV_WIDTH, NS), lambda i: (0, 0)),
                   pl.BlockSpec((POOL_HIST, SB, pool_w), lambda i: (0, i, 0))],
        out_shape=[jax.ShapeDtypeStruct((NS, mixed_w), BF16),
                   jax.ShapeDtypeStruct((KV_WIDTH, NS), F32),
                   jax.ShapeDtypeStruct((KV_WIDTH, NS), F32),
                   jax.ShapeDtypeStruct((POOL_HIST, NS, pool_w), F32)],
        scratch_shapes=[pltpu.VMEM((NS, in_w), F32),
                        pltpu.VMEM((NS, mixed_w), F32)],
        compiler_params=pltpu.CompilerParams(dimension_semantics=("arbitrary",),
                                             vmem_limit_bytes=VMEM_LIMIT_BYTES),
        name="sample_layer",
    )(xs, n1, w_in, kt_cache, vt_cache,
      jnp.transpose(state_pool[0], (1, 0, 2)), jnp.asarray(_sample_bias(L)), sink_rows, w_pool, pscale)

    n_units = (T // BLOCK) * N_KV_HEADS
    tiles_per_seq = S // T
    n_tiles = B * tiles_per_seq

    def mixer_tile(s):
        a = jnp.minimum(s, n_tiles - 1)
        return a // tiles_per_seq, a % tiles_per_seq

    def dense_tile(s):
        a = jnp.maximum(s - 1, 0)
        return a // tiles_per_seq, a % tiles_per_seq

    hbm = pl.BlockSpec(memory_space=pl.ANY)
    assert NS % n_tiles == 0 and NS == L == LANES
    cache_blk = pl.BlockSpec((NS // n_tiles, KV_WIDTH, L), lambda s: (jnp.minimum(s, n_tiles - 1), 0, 0))
    y_prompt, k_last, v_last, pool_last, y_sample, nkt, nvt = pl.pallas_call(
        functools.partial(_prompt_body, tiles_per_seq=tiles_per_seq, n_tiles=n_tiles),
        grid=(n_tiles + 1,),
        in_specs=[pl.BlockSpec((1, T, D), lambda s: (*mixer_tile(s), 0)),
                  pl.BlockSpec((1, T, D), lambda s: (*dense_tile(s), 0)),
                  _resident((NS, 1, D)),
                  _resident((NS, mixed_w)),
                  _resident((1, D)),
                  _resident((2, N_KV_HEADS, 2 * BLOCK, 4 * BLOCK)),
                  pl.BlockSpec(memory_space=pltpu.SMEM),
                  _resident((1, pool_w)),
                  _resident((1, D)),
                  _resident((1, D)),
                  cache_blk, cache_blk,
                  _resident((KV_WIDTH, NS)),
                  _resident((KV_WIDTH, NS)),
                  hbm, hbm, hbm, hbm, hbm, hbm],
        out_specs=[pl.BlockSpec((1, T, D), lambda s: (*dense_tile(s), 0)),
                   pl.BlockSpec((1, KV_WIDTH, WINDOW), lambda s: (mixer_tile(s)[0], 0, 0)),
                   pl.BlockSpec((1, KV_WIDTH, WINDOW), lambda s: (mixer_tile(s)[0], 0, 0)),
                   pl.BlockSpec((1, POOL_HIST, pool_w), lambda s: (mixer_tile(s)[0], 0, 0)),
                   pl.BlockSpec((NS, 1, D), lambda s: (0, 0, 0)),
                   cache_blk, cache_blk],
        out_shape=[jax.ShapeDtypeStruct((B, S, D), F32),
                   jax.ShapeDtypeStruct((B, KV_WIDTH, WINDOW), F32),
                   jax.ShapeDtypeStruct((B, KV_WIDTH, WINDOW), F32),
                   jax.ShapeDtypeStruct((B, POOL_HIST, pool_w), F32),
                   jax.ShapeDtypeStruct((NS, 1, D), F32),
                   jax.ShapeDtypeStruct((NS, KV_WIDTH, L), F32),
                   jax.ShapeDtypeStruct((NS, KV_WIDTH, L), F32)],
        scratch_shapes=[pltpu.VMEM((D, in_w), BF16),
                        pltpu.VMEM((len(POOL_WINDOWS) // 2, 2 * LANES, 2 * LANES), BF16),
                        pltpu.VMEM((mixed_w, D), BF16),
                        pltpu.VMEM((D, d_ff), BF16),
                        pltpu.VMEM((D, d_ff), BF16),
                        pltpu.VMEM((d_ff, D), BF16),
                        pltpu.VMEM(w_pool.shape[1:], F32),
                        pltpu.SemaphoreType.DMA((n_units + 1,)),
                        pltpu.VMEM((KV_WIDTH, T), BF16),
                        pltpu.VMEM((KV_WIDTH, BLOCK), BF16),
                        pltpu.VMEM((4, T + BLOCK, LANES), BF16),
                        pltpu.VMEM((T + HIST_PAD, pool_w), F32),
                        pltpu.VMEM((T, mixed_w), BF16),
                        pltpu.VMEM((T, d_ff), BF16),
                        pltpu.VMEM((n_units, 2 * BLOCK, 4 * BLOCK), F32),
                        pltpu.VMEM((n_units, 2, 2 * BLOCK, LANES), F32),
                        pltpu.VMEM((n_units, 2 * BLOCK, 4 * BLOCK), BF16)],
        compiler_params=pltpu.CompilerParams(dimension_semantics=("arbitrary",),
                                             vmem_limit_bytes=VMEM_LIMIT_BYTES),
        name="prompt_layer",
    )(x_prompt, x_prompt, xs, mix_sample, n1, jnp.asarray(_prompt_bias()), sinks, pscale, n2, fn,
      kt_cache, vt_cache, knew_t, vnew_t, w_in, w_pool, w_out, w_gate, w_up, w_down)

    return (y_prompt, y_sample, cache_from_kernel(k_last), cache_from_kernel(v_last),
            pool_last[None], cache_from_kernel(nkt), cache_from_kernel(nvt),
            jnp.transpose(npool, (1, 0, 2))[None])
```

```python
import functools

import numpy as np
import jax
import jax.numpy as jnp
from jax import lax
from jax.experimental import pallas as pl
from jax.experimental.pallas import tpu as pltpu

N_HEADS = 8
N_KV_HEADS = 2
HEAD_DIM = 64
GQA_GROUP = N_HEADS // N_KV_HEADS
WINDOW = 128
BLOCK = WINDOW
PAST_LEN = 8192
POOL_WINDOWS = (2, 4, 8, 16)
POOL_HIST = max(POOL_WINDOWS) - 1
RMS_EPS = 1e-5
NEG_INF = -1e30
ATTN_SCALE = HEAD_DIM ** -0.5

ATTN_WIDTH = N_HEADS * HEAD_DIM
KV_WIDTH = N_KV_HEADS * HEAD_DIM
N_PAIRS = N_HEADS // 2
LANES = 128
SUBLANES = 8
MXU_DEPTH = 256
HIST_PAD = 16

PROMPT_TILE = 512
SAMPLE_BLOCK = 32
FFN_CHUNK = 512
VMEM_LIMIT_BYTES = 60 * 1024 * 1024

F32 = jnp.float32
BF16 = jnp.bfloat16


def _rms(x, g):
    var = jnp.mean(x * x, axis=-1, keepdims=True)
    return x * lax.rsqrt(var + RMS_EPS) * g


def _dot(a, b):
    return jnp.dot(a, b, preferred_element_type=F32)


def _dot_nt(a, b):
    return lax.dot_general(a, b, (((1,), (1,)), ((), ())), preferred_element_type=F32)


def _ffn_chunks(d_ff):
    return [(c, min(c + FFN_CHUNK, d_ff)) for c in range(0, d_ff, FFN_CHUNK)]


def _dense_open(x, mixed_bf16, wout_ref, n2_ref):
    x1 = x + _dot(mixed_bf16, wout_ref[...])
    return x1, _rms(x1, n2_ref[...]).astype(BF16)


def _ffn_chunk(h2, cols, wg_ref, wu_ref, act_ref):
    gate = _dot(h2, wg_ref[:, cols])
    up = _dot(h2, wu_ref[:, cols])
    act_ref[:, cols] = (jax.nn.silu(gate) * up).astype(BF16)


def _dense_close(x1, act_bf16, wd_ref, fn_ref):
    return _rms(x1 + _dot(act_bf16, wd_ref[...]), fn_ref[...])


def _after(value, anchors):
    acc = anchors[0]
    for a in anchors[1:]:
        acc = acc + a
    bits = lax.bitcast_convert_type(acc, jnp.int32)
    zero = lax.shift_right_logical(lax.shift_right_logical(bits, 16), 16)[0:1, 0:1]
    head = value[:, 0:MXU_DEPTH] + zero.astype(value.dtype)
    return jnp.concatenate([head, value[:, MXU_DEPTH:]], axis=1)


def _split_heads_to_lane_halves(a):
    lo = lax.broadcasted_iota(jnp.int32, a.shape, 1) < HEAD_DIM
    zero = jnp.zeros_like(a)
    ar = pltpu.roll(a, HEAD_DIM, axis=1)
    return (jnp.where(lo, a, zero).astype(BF16), jnp.where(lo, zero, ar).astype(BF16),
            jnp.where(lo, ar, zero).astype(BF16), jnp.where(lo, zero, a).astype(BF16))


def _load_weights_as_bf16(weights, ring_ref, sem):
    n_slots, rows, max_cols = ring_ref.shape
    chunks = [(src, dst, r0, c0, min(max_cols, dst.shape[1] - c0))
              for src, dst in weights
              for r0 in range(0, dst.shape[0], rows)
              for c0 in range(0, dst.shape[1], max_cols)]

    def chunk_copy(i):
        src, _, r0, c0, cols = chunks[i]
        slot = i % n_slots
        return pltpu.make_async_copy(src.at[0, pl.ds(r0, rows), pl.ds(c0, cols)],
                                     ring_ref.at[slot, :, 0:cols], sem.at[slot])

    for i in range(min(n_slots - 1, len(chunks))):
        chunk_copy(i).start()
    for i, (_, dst, r0, c0, cols) in enumerate(chunks):
        nxt = i + n_slots - 1
        if nxt < len(chunks):
            chunk_copy(nxt).start()
        chunk_copy(i).wait()
        dst[r0:r0 + rows, c0:c0 + cols] = ring_ref[i % n_slots, :, 0:cols].astype(BF16)


def _prompt_body(x_ref, xd_ref, xs_ref, mixs_ref, n1_ref, bias_ref, sink_ref, pscale_ref, n2_ref, fn_ref,
                 kc_ref, vc_ref, knewt_ref, vnewt_ref,
                 win_hbm, wpool_hbm, wout_hbm, wg_hbm, wu_hbm, wd_hbm,
                 y_ref, kout_ref, vout_ref, pout_ref, ys_ref, nkc_ref, nvc_ref,
                 win_ref, wpool_ref, wout_ref, wg_ref, wu_ref, wd_ref, pool_stage_ref, dma_sem,
                 kt_ref, kprev_ref, vv_ref, ubuf_ref, mixed_ref, act_ref, s_ref, m_ref, p_ref,
                 *, tiles_per_seq, n_tiles):
    T = x_ref.shape[1]
    NS = xs_ref.shape[0]
    step = pl.program_id(0)
    t = jnp.minimum(step, n_tiles - 1) % tiles_per_seq
    nblk = T // BLOCK
    ROWS = 2 * BLOCK
    KEYS = 2 * BLOCK

    @pl.when(t == 0)
    def _():
        kprev_ref[...] = jnp.zeros(kprev_ref.shape, BF16)
        vv_ref[:, 0:BLOCK, :] = jnp.zeros((4, BLOCK, LANES), BF16)
        ubuf_ref[0:HIST_PAD, :] = jnp.zeros((HIST_PAD, ubuf_ref.shape[1]), F32)

    def update_sample_caches():
        CS, _, L = kc_ref.shape
        last_lane = lax.broadcasted_iota(jnp.int32, (KV_WIDTH, L), 1) == L - 1
        first_sample = jnp.minimum(step, n_tiles - 1) * CS
        for new_t_ref, cache_ref, out_ref in ((knewt_ref, kc_ref, nkc_ref), (vnewt_ref, vc_ref, nvc_ref)):
            new_t = new_t_ref[...]
            for i in range(CS):
                new_col = pltpu.roll(new_t, L - 1 - (first_sample + i), axis=1)
                out_ref[i] = jnp.where(last_lane, new_col, pltpu.roll(cache_ref[i], L - 1, axis=1))

    def mixer():
        return _mixer_phases(x_ref, n1_ref, win_ref, bias_ref, sink_ref, wpool_ref, pscale_ref,
                             kout_ref, vout_ref, pout_ref, kt_ref, kprev_ref, vv_ref, ubuf_ref, mixed_ref,
                             s_ref, m_ref, p_ref, t)

    chunks = [slice(c0, c1) for c0, c1 in _ffn_chunks(wg_ref.shape[1])]

    @pl.when(step == 0)
    def _():
        pool_copy = pltpu.make_async_copy(wpool_hbm.at[0], pool_stage_ref, dma_sem.at[s_ref.shape[0]])
        pool_copy.start()
        _load_weights_as_bf16([(win_hbm, win_ref), (wout_hbm, wout_ref), (wg_hbm, wg_ref), (wu_hbm, wu_ref),
                               (wd_hbm, wd_ref)], s_ref, dma_sem)
        pool_copy.wait()
        zp = jnp.zeros((LANES, LANES), BF16)
        for i in range(wpool_ref.shape[0]):
            wa = pool_stage_ref[2 * i].astype(BF16)
            wb = pool_stage_ref[2 * i + 1].astype(BF16)
            wpool_ref[i] = jnp.concatenate([jnp.concatenate([wa, zp], axis=1),
                                            jnp.concatenate([zp, wb], axis=1)], axis=0)
        update_sample_caches()
        sample_act = act_ref.at[0:NS]
        x1, h2 = _dense_open(xs_ref[:, 0, :], mixs_ref[...], wout_ref, n2_ref)
        for cols in chunks:
            _ffn_chunk(h2, cols, wg_ref, wu_ref, sample_act)
        ys_ref[:, 0, :] = _dense_close(x1, sample_act[...], wd_ref, fn_ref)
        for _ in mixer():
            pass

    @pl.when(step == n_tiles)
    def _():
        x1, h2 = _dense_open(xd_ref[0], mixed_ref[...], wout_ref, n2_ref)
        for cols in chunks:
            _ffn_chunk(h2, cols, wg_ref, wu_ref, act_ref)
        y_ref[0] = _dense_close(x1, act_ref[...], wd_ref, fn_ref)

    @pl.when(jnp.logical_and(step > 0, step < n_tiles))
    def _():
        update_sample_caches()
        x1, h2 = _dense_open(xd_ref[0], mixed_ref[...], wout_ref, n2_ref)
        assert len(chunks) == 6
        phases = mixer()
        proj_done = next(phases)
        _ffn_chunk(h2, chunks[0], wg_ref, wu_ref, act_ref)
        _ffn_chunk(h2, chunks[1], wg_ref, wu_ref, act_ref)
        _ffn_chunk(_after(h2, proj_done), chunks[2], wg_ref, wu_ref, act_ref)
        score_done = next(phases)
        _ffn_chunk(h2, chunks[3], wg_ref, wu_ref, act_ref)
        _ffn_chunk(_after(h2, score_done), chunks[4], wg_ref, wu_ref, act_ref)
        exp_done = next(phases)
        _ffn_chunk(_after(h2, exp_done), chunks[5], wg_ref, wu_ref, act_ref)
        next(phases)
        y_ref[0] = _dense_close(x1, act_ref[...], wd_ref, fn_ref)


def _mixer_phases(x_ref, n1_ref, win_ref, bias_ref, sink_ref, wpool_ref, pscale_ref,
                  kout_ref, vout_ref, pout_ref, kt_ref, kprev_ref, vv_ref, ubuf_ref, mixed_ref,
                  s_ref, m_ref, p_ref, t):
    T = x_ref.shape[1]
    nblk = T // BLOCK
    ROWS = 2 * BLOCK
    KEYS = 2 * BLOCK

    x = x_ref[0]
    h = _rms(x, n1_ref[...]).astype(BF16)
    proj = _dot(h, win_ref[...])
    yield [proj[T - SUBLANES:, proj.shape[1] - LANES:]]
    q = (proj[:, 0:ATTN_WIDTH] * ATTN_SCALE).astype(BF16)
    k = proj[:, ATTN_WIDTH:ATTN_WIDTH + KV_WIDTH]
    v = proj[:, ATTN_WIDTH + KV_WIDTH:ATTN_WIDTH + 2 * KV_WIDTH]
    u = proj[:, ATTN_WIDTH + 2 * KV_WIDTH:]

    ubuf_ref[HIST_PAD:, :] = u
    pout_ref[0] = ubuf_ref[HIST_PAD + T - POOL_HIST:HIST_PAD + T, :]

    k_t = k.T
    kt_ref[...] = k_t.astype(BF16)
    kout_ref[0] = k_t[:, T - WINDOW:]
    vout_ref[0] = v[T - WINDOW:, :].T
    for idx, part in enumerate(_split_heads_to_lane_halves(v)):
        vv_ref[idx, BLOCK:, :] = part

    first_tile = (t == 0).astype(jnp.int32)
    score_done = []
    no_keys = jnp.zeros((HEAD_DIM, KEYS), BF16)
    for j in range(nblk):
        rows = slice(j * BLOCK, (j + 1) * BLOCK)
        for g in range(N_KV_HEADS):
            q2 = jnp.concatenate([q[rows, (2 * g) * LANES:(2 * g + 1) * LANES],
                                  q[rows, (2 * g + 1) * LANES:(2 * g + 2) * LANES]], axis=0)
            dims = slice(g * HEAD_DIM, (g + 1) * HEAD_DIM)
            if j == 0:
                kg = jnp.concatenate([kprev_ref[dims, :], kt_ref[dims, 0:BLOCK]], axis=1)
            else:
                kg = kt_ref[dims, (j - 1) * BLOCK:(j + 1) * BLOCK]
            kmat = jnp.concatenate([jnp.concatenate([kg, no_keys], axis=1),
                                    jnp.concatenate([no_keys, kg], axis=1)], axis=0)
            bias = bias_ref[first_tile if j == 0 else 0, g]
            sc = _dot(q2, kmat) + bias
            s_ref[j * N_KV_HEADS + g] = sc
            score_done.append(sc[ROWS - SUBLANES:, 2 * KEYS - LANES:])
    yield score_done

    top = lax.broadcasted_iota(jnp.int32, (ROWS, LANES), 0) < BLOCK
    sink_b = [[jnp.where(top, sink_ref[GQA_GROUP * g + c], sink_ref[GQA_GROUP * g + 2 + c]) for c in range(2)]
              for g in range(N_KV_HEADS)]
    n_units = nblk * N_KV_HEADS
    for un in range(n_units):
        for c in range(2):
            sh = s_ref[un, :, c * KEYS:(c + 1) * KEYS]
            m_ref[un, c] = jnp.maximum(jnp.max(sh, axis=-1, keepdims=True), sink_b[un % N_KV_HEADS][c])
    exp_done = []
    for un in range(n_units):
        for c in range(2):
            sh = s_ref[un, :, c * KEYS:(c + 1) * KEYS]
            m = m_ref[un, c]
            e = jnp.exp(sh - jnp.concatenate([m, m], axis=1))
            p_ref[un, :, c * KEYS:(c + 1) * KEYS] = e.astype(BF16)
            exp_done.append(e[ROWS - SUBLANES:, KEYS - LANES:])
    yield exp_done

    lo = lax.broadcasted_iota(jnp.int32, (ROWS, LANES), 1) < HEAD_DIM
    first_head_rows = lax.broadcasted_iota(jnp.int32, (2 * KEYS, LANES), 0) < KEYS
    first_head_cols = lax.broadcasted_iota(jnp.int32, (2 * KEYS, LANES), 1) < HEAD_DIM
    ones_cols = jnp.where(first_head_rows == first_head_cols, 1.0, 0.0).astype(BF16)
    for un in range(n_units):
        j, g = divmod(un, N_KV_HEADS)
        keys = slice(j * BLOCK, j * BLOCK + KEYS)
        vmat = jnp.concatenate([vv_ref[2 * g, keys, :], vv_ref[2 * g + 1, keys, :]], axis=0)
        o = _dot(p_ref[un], jnp.concatenate([vmat, ones_cols], axis=1))
        m_sel = jnp.where(lo, m_ref[un, 0], m_ref[un, 1])
        sink_sel = jnp.where(lo, sink_b[g][0], sink_b[g][1])
        out = o[:, 0:LANES] / (o[:, LANES:2 * LANES] + jnp.exp(sink_sel - m_sel))
        rows = slice(j * BLOCK, (j + 1) * BLOCK)
        mixed_ref[rows, (2 * g) * LANES:(2 * g + 1) * LANES] = out[0:BLOCK].astype(BF16)
        mixed_ref[rows, (2 * g + 1) * LANES:(2 * g + 2) * LANES] = out[BLOCK:ROWS].astype(BF16)

    pos = lax.broadcasted_iota(jnp.int32, (T, 1), 0) + t * T
    pooled = []
    for g, w in enumerate(POOL_WINDOWS):
        cols = slice(g * LANES, (g + 1) * LANES)
        cur = ubuf_ref[HIST_PAD:HIST_PAD + T, cols]
        win = cur
        for back in range(1, w):
            win = win + ubuf_ref[HIST_PAD - back:HIST_PAD - back + T, cols]
        count = jnp.minimum(w, pos + 1).astype(F32)
        pooled.append((win / count - cur).astype(BF16))
    for g2 in range(len(POOL_WINDOWS) // 2):
        cols = slice(2 * g2 * LANES, (2 * g2 + 2) * LANES)
        m2 = jnp.concatenate(pooled[2 * g2:2 * g2 + 2], axis=1)
        yg = _dot(m2, wpool_ref[g2]) * pscale_ref[:, cols]
        mixed_ref[:, ATTN_WIDTH + 2 * g2 * LANES:ATTN_WIDTH + (2 * g2 + 2) * LANES] = yg.astype(BF16)

    kprev_ref[...] = kt_ref[:, T - BLOCK:T]
    vv_ref[:, 0:BLOCK, :] = vv_ref[:, T:T + BLOCK, :]
    ubuf_ref[0:HIST_PAD, :] = ubuf_ref[T:T + HIST_PAD, :]
    yield None


def _sample_body(xs_ref, n1_ref, win_ref, kt_ref, vt_ref, hist_ref, bias_ref, sink_ref, wpool_ref, pscale_ref,
                 mix_ref, knewt_ref, vnewt_ref, npool_ref,
                 proj_ref, mixed_ref):
    SB = kt_ref.shape[0]
    L = kt_ref.shape[2]
    i = pl.program_id(0)

    @pl.when(i == 0)
    def _():
        h = _rms(xs_ref[:, 0, :], n1_ref[...]).astype(BF16)
        proj = _dot(h, win_ref[0].astype(BF16))
        proj_ref[...] = proj
        knewt_ref[...] = proj[:, ATTN_WIDTH:ATTN_WIDTH + KV_WIDTH].T
        vnewt_ref[...] = proj[:, ATTN_WIDTH + KV_WIDTH:ATTN_WIDTH + 2 * KV_WIDTH].T

    r0 = pl.multiple_of(i * SB, SB)
    pr = proj_ref[pl.ds(r0, SB), :]
    q = pr[:, 0:ATTN_WIDTH] * ATTN_SCALE
    knew = pr[:, ATTN_WIDTH:ATTN_WIDTH + KV_WIDTH]
    vnew = pr[:, ATTN_WIDTH + KV_WIDTH:ATTN_WIDTH + 2 * KV_WIDTH]
    u = pr[:, ATTN_WIDTH + 2 * KV_WIDTH:]

    npool_ref[0:POOL_HIST - 1] = hist_ref[1:POOL_HIST]
    npool_ref[POOL_HIST - 1] = u

    lo_blk = lax.broadcasted_iota(jnp.int32, (SB, LANES), 1) < HEAD_DIM
    zero_blk = jnp.zeros((SB, LANES), F32)
    qh = []
    for p in range(N_PAIRS):
        pair = q[:, p * LANES:(p + 1) * LANES]
        swapped = pltpu.roll(pair, HEAD_DIM, axis=1)
        if (2 * p) // GQA_GROUP == 0:
            qh += [jnp.where(lo_blk, pair, zero_blk), jnp.where(lo_blk, swapped, zero_blk)]
        else:
            qh += [jnp.where(lo_blk, zero_blk, swapped), jnp.where(lo_blk, zero_blk, pair)]

    rows64 = N_HEADS * SUBLANES
    sub = lax.broadcasted_iota(jnp.int32, (rows64, LANES), 0) % SUBLANES
    lo = lax.broadcasted_iota(jnp.int32, (SUBLANES, LANES), 1) < HEAD_DIM
    n_groups = SB // SUBLANES
    groups = [slice(r * SUBLANES, (r + 1) * SUBLANES) for r in range(n_groups)]
    lhs = jnp.concatenate([qh[h][grp] for grp in groups for h in range(N_HEADS)], axis=0)
    lhs_bf = lhs.astype(BF16)
    scores = []
    for r in range(n_groups):
        s = jnp.zeros((rows64, L), F32)
        for si in range(SUBLANES):
            kb = kt_ref[r * SUBLANES + si].astype(BF16)
            s = jnp.where(sub == si, _dot(lhs_bf[r * rows64:(r + 1) * rows64], kb), s)
        scores.append(s)
    s = jnp.concatenate(scores, axis=0) + jnp.concatenate([bias_ref[...]] * n_groups, axis=0)
    sink = jnp.concatenate([jnp.full((SUBLANES, 1), sink_ref[h], F32) for h in range(N_HEADS)] * n_groups, axis=0)
    knew_rep = jnp.concatenate([knew[grp] for grp in groups for _ in range(N_HEADS)], axis=0)
    vnew_rep = jnp.concatenate([vnew[grp] for grp in groups for _ in range(N_HEADS)], axis=0)
    s_new = jnp.sum(lhs * knew_rep, axis=-1, keepdims=True)
    m = jnp.maximum(jnp.maximum(jnp.max(s, axis=-1, keepdims=True), s_new), sink)
    e = jnp.exp(s - m)
    e_new = jnp.exp(s_new - m)
    denom = jnp.sum(e, axis=-1, keepdims=True) + e_new + jnp.exp(sink - m)
    e_bf = e.astype(BF16)
    outs = []
    for r in range(n_groups):
        o = jnp.zeros((rows64, LANES), F32)
        for si in range(SUBLANES):
            vb = vt_ref[r * SUBLANES + si].astype(BF16)
            o = jnp.where(sub == si, _dot_nt(e_bf[r * rows64:(r + 1) * rows64], vb), o)
        outs.append(o)
    o_all = (jnp.concatenate(outs, axis=0) + e_new * vnew_rep) / denom
    for r in range(n_groups):
        o = o_all[r * rows64:(r + 1) * rows64]
        out_rows = pl.ds(pl.multiple_of(r0 + r * SUBLANES, SUBLANES), SUBLANES)
        for p in range(N_PAIRS):
            a = o[2 * p * SUBLANES:(2 * p + 1) * SUBLANES]
            b = o[(2 * p + 1) * SUBLANES:(2 * p + 2) * SUBLANES]
            if (2 * p) // GQA_GROUP == 0:
                pair = jnp.where(lo, a, pltpu.roll(b, HEAD_DIM, axis=1))
            else:
                pair = jnp.where(lo, pltpu.roll(a, HEAD_DIM, axis=1), b)
            mixed_ref[out_rows, p * LANES:(p + 1) * LANES] = pair

    blk_rows = pl.ds(r0, SB)
    for g, w in enumerate(POOL_WINDOWS):
        cols = slice(g * LANES, (g + 1) * LANES)
        cur = u[:, cols]
        win = cur
        for back in range(1, w):
            win = win + hist_ref[POOL_HIST - back, :, cols]
        count = float(min(w, PAST_LEN + 1))
        m = (win / count - cur).astype(BF16)
        yg = _dot(m, wpool_ref[0, g].astype(BF16)) * pscale_ref[:, cols]
        mixed_ref[blk_rows, ATTN_WIDTH + g * LANES:ATTN_WIDTH + (g + 1) * LANES] = yg

    @pl.when(i == pl.num_programs(0) - 1)
    def _():
        mix_ref[...] = mixed_ref[...].astype(BF16)


def _alibi_slopes():
    return np.exp2(-8.0 * np.arange(1, N_HEADS + 1, dtype=np.float32) / N_HEADS).astype(np.float32)


def _prompt_bias():
    slopes = _alibi_slopes()
    qi = np.arange(BLOCK)[:, None]
    ki = np.arange(2 * BLOCK)[None, :]
    rel = BLOCK + qi - ki
    band = (rel >= 0) & (rel <= WINDOW)
    out = np.empty((2, N_KV_HEADS, 2 * BLOCK, 4 * BLOCK), np.float32)
    for first in range(2):
        valid = band & ((ki >= BLOCK) if first else True)
        for g in range(N_KV_HEADS):
            for r in range(2):
                for c in range(2):
                    slope = slopes[GQA_GROUP * g + 2 * r + c]
                    val = np.where(valid, -slope * rel.astype(np.float32), np.float32(NEG_INF))
                    out[first, g, r * BLOCK:(r + 1) * BLOCK, c * 2 * BLOCK:(c + 1) * 2 * BLOCK] = val
    return out


def _sample_bias(L):
    assert PAST_LEN >= L and L <= WINDOW
    slopes = _alibi_slopes()
    rel = (L - np.arange(L)).astype(np.float32)
    return np.repeat(-slopes[:, None] * rel[None, :], SUBLANES, axis=0).astype(np.float32)


def _resident(shape):
    zeros = (0,) * len(shape)
    return pl.BlockSpec(shape, lambda *_: zeros, pipeline_mode=pl.Buffered(1))


def kernel(x_prompt, x_sample, cache_k_window, cache_v_window, state_pool, norm1, w_in, attn_sinks, w_pool,
           pool_scale, w_out, norm2, w_gate, w_up, w_down, final_norm):
    B, S, D = x_prompt.shape
    NS = x_sample.shape[0]
    L = cache_k_window.shape[2]
    d_ff = w_gate.shape[-1]
    pool_w = state_pool.shape[-1]
    assert norm1.shape[0] == 1 and x_sample.shape[1] == 1
    assert w_in.shape[-1] == ATTN_WIDTH + 2 * KV_WIDTH + pool_w and pool_w == len(POOL_WINDOWS) * LANES
    assert S % PROMPT_TILE == 0 and NS % SAMPLE_BLOCK == 0 and state_pool.shape[2] == POOL_HIST
    T = PROMPT_TILE

    assert D % (2 * BLOCK) == 0 and d_ff % (2 * BLOCK) == 0 and NS <= T
    in_w = w_in.shape[-1]
    n1 = norm1[0].reshape(1, D)
    n2 = norm2[0].reshape(1, D)
    fn = final_norm.reshape(1, D)
    pscale = pool_scale[0].reshape(1, pool_w)
    sinks = attn_sinks[0]
    xs = x_sample
    mixed_w = ATTN_WIDTH + pool_w

    def cache_to_kernel(c):
        return jnp.transpose(c[0], (0, 2, 3, 1)).reshape(NS, KV_WIDTH, L)

    def cache_from_kernel(c):
        n, _, keys = c.shape
        return jnp.transpose(c.reshape(n, N_KV_HEADS, HEAD_DIM, keys), (0, 3, 1, 2))[None]

    SB = SAMPLE_BLOCK
    kt_cache = cache_to_kernel(cache_k_window)
    vt_cache = cache_to_kernel(cache_v_window)
    mix_sample, knew_t, vnew_t, npool = pl.pallas_call(
        _sample_body,
        grid=(NS // SB,),
        in_specs=[_resident((NS, 1, D)),
                  _resident((1, D)),
                  _resident(w_in.shape),
                  pl.BlockSpec((SB, KV_WIDTH, L), lambda i: (i, 0, 0)),
                  pl.BlockSpec((SB, KV_WIDTH, L), lambda i: (i, 0, 0)),
                  pl.BlockSpec((POOL_HIST, SB, pool_w), lambda i: (0, i, 0)),
                  _resident((N_HEADS * SUBLANES, L)),
                  pl.BlockSpec(memory_space=pltpu.SMEM),
                  _resident(w_pool.shape),
                  _resident((1, pool_w))],
        out_specs=[pl.BlockSpec((NS, mixed_w), lambda i: (0, 0)),
                   pl.BlockSpec((KV_WIDTH, NS), lambda i: (0, 0)),
                   pl.BlockSpec((KV_WIDTH, NS), lambda i: (0, 0)),
                   pl.BlockSpec((POOL_HIST, SB, pool_w), lambda i: (0, i, 0))],
        out_shape=[jax.ShapeDtypeStruct((NS, mixed_w), BF16),
                   jax.ShapeDtypeStruct((KV_WIDTH, NS), F32),
                   jax.ShapeDtypeStruct((KV_WIDTH, NS), F32),
                   jax.ShapeDtypeStruct((POOL_HIST, NS, pool_w), F32)],
        scratch_shapes=[pltpu.VMEM((NS, in_w), F32),
                        pltpu.VMEM((NS, mixed_w), F32)],
        compiler_params=pltpu.CompilerParams(dimension_semantics=("arbitrary",),
                                             vmem_limit_bytes=VMEM_LIMIT_BYTES),
        name="sample_layer",
    )(xs, n1, w_in, kt_cache, vt_cache,
      jnp.transpose(state_pool[0], (1, 0, 2)), jnp.asarray(_sample_bias(L)), sinks, w_pool, pscale)

    n_units = (T // BLOCK) * N_KV_HEADS
    tiles_per_seq = S // T
    n_tiles = B * tiles_per_seq

    def mixer_tile(s):
        a = jnp.minimum(s, n_tiles - 1)
        return a // tiles_per_seq, a % tiles_per_seq

    def dense_tile(s):
        a = jnp.maximum(s - 1, 0)
        return a // tiles_per_seq, a % tiles_per_seq

    hbm = pl.BlockSpec(memory_space=pl.ANY)
    assert NS % n_tiles == 0 and NS == L == LANES
    cache_blk = pl.BlockSpec((NS // n_tiles, KV_WIDTH, L), lambda s: (jnp.minimum(s, n_tiles - 1), 0, 0))
    y_prompt, k_last, v_last, pool_last, y_sample, nkt, nvt = pl.pallas_call(
        functools.partial(_prompt_body, tiles_per_seq=tiles_per_seq, n_tiles=n_tiles),
        grid=(n_tiles + 1,),
        in_specs=[pl.BlockSpec((1, T, D), lambda s: (*mixer_tile(s), 0)),
                  pl.BlockSpec((1, T, D), lambda s: (*dense_tile(s), 0)),
                  _resident((NS, 1, D)),
                  _resident((NS, mixed_w)),
                  _resident((1, D)),
                  _resident((2, N_KV_HEADS, 2 * BLOCK, 4 * BLOCK)),
                  pl.BlockSpec(memory_space=pltpu.SMEM),
                  _resident((1, pool_w)),
                  _resident((1, D)),
                  _resident((1, D)),
                  cache_blk, cache_blk,
                  _resident((KV_WIDTH, NS)),
                  _resident((KV_WIDTH, NS)),
                  hbm, hbm, hbm, hbm, hbm, hbm],
        out_specs=[pl.BlockSpec((1, T, D), lambda s: (*dense_tile(s), 0)),
                   pl.BlockSpec((1, KV_WIDTH, WINDOW), lambda s: (mixer_tile(s)[0], 0, 0)),
                   pl.BlockSpec((1, KV_WIDTH, WINDOW), lambda s: (mixer_tile(s)[0], 0, 0)),
                   pl.BlockSpec((1, POOL_HIST, pool_w), lambda s: (mixer_tile(s)[0], 0, 0)),
                   pl.BlockSpec((NS, 1, D), lambda s: (0, 0, 0)),
                   cache_blk, cache_blk],
        out_shape=[jax.ShapeDtypeStruct((B, S, D), F32),
                   jax.ShapeDtypeStruct((B, KV_WIDTH, WINDOW), F32),
                   jax.ShapeDtypeStruct((B, KV_WIDTH, WINDOW), F32),
                   jax.ShapeDtypeStruct((B, POOL_HIST, pool_w), F32),
                   jax.ShapeDtypeStruct((NS, 1, D), F32),
                   jax.ShapeDtypeStruct((NS, KV_WIDTH, L), F32),
                   jax.ShapeDtypeStruct((NS, KV_WIDTH, L), F32)],
        scratch_shapes=[pltpu.VMEM((D, in_w), BF16),
                        pltpu.VMEM((len(POOL_WINDOWS) // 2, 2 * LANES, 2 * LANES), BF16),
                        pltpu.VMEM((mixed_w, D), BF16),
                        pltpu.VMEM((D, d_ff), BF16),
                        pltpu.VMEM((D, d_ff), BF16),
                        pltpu.VMEM((d_ff, D), BF16),
                        pltpu.VMEM(w_pool.shape[1:], F32),
                        pltpu.SemaphoreType.DMA((n_units + 1,)),
                        pltpu.VMEM((KV_WIDTH, T), BF16),
                        pltpu.VMEM((KV_WIDTH, BLOCK), BF16),
                        pltpu.VMEM((4, T + BLOCK, LANES), BF16),
                        pltpu.VMEM((T + HIST_PAD, pool_w), F32),
                        pltpu.VMEM((T, mixed_w), BF16),
                        pltpu.VMEM((T, d_ff), BF16),
                        pltpu.VMEM((n_units, 2 * BLOCK, 4 * BLOCK), F32),
                        pltpu.VMEM((n_units, 2, 2 * BLOCK, LANES), F32),
                        pltpu.VMEM((n_units, 2 * BLOCK, 4 * BLOCK), BF16)],
        compiler_params=pltpu.CompilerParams(dimension_semantics=("arbitrary",),
                                             vmem_limit_bytes=VMEM_LIMIT_BYTES),
        name="prompt_layer",
    )(x_prompt, x_prompt, xs, mix_sample, n1, jnp.asarray(_prompt_bias()), sinks, pscale, n2, fn,
      kt_cache, vt_cache, knew_t, vnew_t, w_in, w_pool, w_out, w_gate, w_up, w_down)

    return (y_prompt, y_sample, cache_from_kernel(k_last), cache_from_kernel(v_last),
            pool_last[None], cache_from_kernel(nkt), cache_from_kernel(nvt),
            jnp.transpose(npool, (1, 0, 2))[None])
```

```python
import functools

import numpy as np
import jax
import jax.numpy as jnp
from jax import lax
from jax.experimental import pallas as pl
from jax.experimental.pallas import tpu as pltpu

N_HEADS = 8
N_KV_HEADS = 2
HEAD_DIM = 64
GQA_GROUP = N_HEADS // N_KV_HEADS
WINDOW = 128
BLOCK = WINDOW
PAST_LEN = 8192
POOL_WINDOWS = (2, 4, 8, 16)
POOL_HIST = max(POOL_WINDOWS) - 1
RMS_EPS = 1e-5
NEG_INF = -1e30
ATTN_SCALE = HEAD_DIM ** -0.5

ATTN_WIDTH = N_HEADS * HEAD_DIM
KV_WIDTH = N_KV_HEADS * HEAD_DIM
N_PAIRS = N_HEADS // 2
LANES = 128
SUBLANES = 8
MXU_DEPTH = 256
HIST_PAD = 16

PROMPT_TILE = 512
SAMPLE_BLOCK = 32
FFN_CHUNK = 512
VMEM_LIMIT_BYTES = 62 * 1024 * 1024

F32 = jnp.float32
BF16 = jnp.bfloat16


def _rms(x, g):
    var = jnp.mean(x * x, axis=-1, keepdims=True)
    return x * lax.rsqrt(var + RMS_EPS) * g


def _dot(a, b):
    return jnp.dot(a, b, preferred_element_type=F32)


def _dot_nt(a, b):
    return lax.dot_general(a, b, (((1,), (1,)), ((), ())), preferred_element_type=F32)


def _ffn_chunks(d_ff):
    return [(c, min(c + FFN_CHUNK, d_ff)) for c in range(0, d_ff, FFN_CHUNK)]


def _dense_open(x, mixed_bf16, wout_ref, n2_ref):
    x1 = x + _dot(mixed_bf16, wout_ref[...])
    return x1, _rms(x1, n2_ref[...]).astype(BF16)


def _ffn_chunk(h2, cols, wg_ref, wu_ref, act_ref):
    gate = _dot(h2, wg_ref[:, cols])
    up = _dot(h2, wu_ref[:, cols])
    act_ref[:, cols] = (jax.nn.silu(gate) * up).astype(BF16)


def _dense_close(x1, act_bf16, wd_ref, fn_ref):
    return _rms(x1 + _dot(act_bf16, wd_ref[...]), fn_ref[...])


def _after(value, anchors):
    acc = anchors[0]
    for a in anchors[1:]:
        acc = acc + a
    bits = lax.bitcast_convert_type(acc, jnp.int32)
    zero = lax.shift_right_logical(lax.shift_right_logical(bits, 16), 16)[0:1, 0:1]
    head = value[:, 0:MXU_DEPTH] + zero.astype(value.dtype)
    return jnp.concatenate([head, value[:, MXU_DEPTH:]], axis=1)


def _split_heads_to_lane_halves(a):
    lo = lax.broadcasted_iota(jnp.int32, a.shape, 1) < HEAD_DIM
    zero = jnp.zeros_like(a)
    ar = pltpu.roll(a, HEAD_DIM, axis=1)
    return (jnp.where(lo, a, zero).astype(BF16), jnp.where(lo, zero, ar).astype(BF16),
            jnp.where(lo, ar, zero).astype(BF16), jnp.where(lo, zero, a).astype(BF16))


def _load_weights_as_bf16(weights, ring_ref, sem):
    n_slots, rows, max_cols = ring_ref.shape
    chunks = [(src, dst, r0, c0, min(max_cols, dst.shape[1] - c0))
              for src, dst in weights
              for r0 in range(0, dst.shape[0], rows)
              for c0 in range(0, dst.shape[1], max_cols)]

    def chunk_copy(i):
        src, _, r0, c0, cols = chunks[i]
        slot = i % n_slots
        return pltpu.make_async_copy(src.at[0, pl.ds(r0, rows), pl.ds(c0, cols)],
                                     ring_ref.at[slot, :, 0:cols], sem.at[slot])

    for i in range(min(n_slots - 1, len(chunks))):
        chunk_copy(i).start()
    for i, (_, dst, r0, c0, cols) in enumerate(chunks):
        nxt = i + n_slots - 1
        if nxt < len(chunks):
            chunk_copy(nxt).start()
        chunk_copy(i).wait()
        dst[r0:r0 + rows, c0:c0 + cols] = ring_ref[i % n_slots, :, 0:cols].astype(BF16)


def _prompt_body(x_ref, xd_ref, xs_ref, mixs_ref, n1_ref, bias_ref, sink_ref, pscale_ref, n2_ref, fn_ref,
                 kc_ref, vc_ref, knewt_ref, vnewt_ref,
                 win_hbm, wpool_hbm, wout_hbm, wg_hbm, wu_hbm, wd_hbm,
                 y_ref, kout_ref, vout_ref, pout_ref, ys_ref, nkc_ref, nvc_ref,
                 win_ref, wpool_ref, wout_ref, wg_ref, wu_ref, wd_ref, pool_stage_ref, dma_sem,
                 kt_ref, kprev_ref, vv_ref, ubuf_ref, mixed_ref, act_ref, s_ref, m_ref, p_ref,
                 *, tiles_per_seq, n_tiles):
    T = x_ref.shape[1]
    NS = xs_ref.shape[0]
    step = pl.program_id(0)
    t = jnp.minimum(step, n_tiles - 1) % tiles_per_seq
    nblk = T // BLOCK
    ROWS = 2 * BLOCK
    KEYS = 2 * BLOCK

    @pl.when(t == 0)
    def _():
        kprev_ref[...] = jnp.zeros(kprev_ref.shape, BF16)
        vv_ref[:, 0:BLOCK, :] = jnp.zeros((4, BLOCK, LANES), BF16)
        ubuf_ref[0:HIST_PAD, :] = jnp.zeros((HIST_PAD, ubuf_ref.shape[1]), F32)

    def update_sample_caches():
        CS, _, L = kc_ref.shape
        last_lane = lax.broadcasted_iota(jnp.int32, (KV_WIDTH, L), 1) == L - 1
        first_sample = jnp.minimum(step, n_tiles - 1) * CS
        for new_t_ref, cache_ref, out_ref in ((knewt_ref, kc_ref, nkc_ref), (vnewt_ref, vc_ref, nvc_ref)):
            new_t = new_t_ref[...]
            for i in range(CS):
                new_col = pltpu.roll(new_t, L - 1 - (first_sample + i), axis=1)
                out_ref[i] = jnp.where(last_lane, new_col, pltpu.roll(cache_ref[i], L - 1, axis=1))

    def mixer():
        return _mixer_phases(x_ref, n1_ref, win_ref, bias_ref, sink_ref, wpool_ref, pscale_ref,
                             kout_ref, vout_ref, pout_ref, kt_ref, kprev_ref, vv_ref, ubuf_ref, mixed_ref,
                             s_ref, m_ref, p_ref, t)

    chunks = [slice(c0, c1) for c0, c1 in _ffn_chunks(wg_ref.shape[1])]

    @pl.when(step == 0)
    def _():
        pool_copy = pltpu.make_async_copy(wpool_hbm.at[0], pool_stage_ref, dma_sem.at[s_ref.shape[0]])
        pool_copy.start()
        _load_weights_as_bf16([(win_hbm, win_ref), (wout_hbm, wout_ref), (wg_hbm, wg_ref), (wu_hbm, wu_ref),
                               (wd_hbm, wd_ref)], s_ref, dma_sem)
        pool_copy.wait()
        zp = jnp.zeros((LANES, LANES), BF16)
        for i in range(wpool_ref.shape[0]):
            wa = pool_stage_ref[2 * i].astype(BF16)
            wb = pool_stage_ref[2 * i + 1].astype(BF16)
            wpool_ref[i] = jnp.concatenate([jnp.concatenate([wa, zp], axis=1),
                                            jnp.concatenate([zp, wb], axis=1)], axis=0)
        update_sample_caches()
        sample_act = act_ref.at[0:NS]
        x1, h2 = _dense_open(xs_ref[:, 0, :], mixs_ref[...], wout_ref, n2_ref)
        for cols in chunks:
            _ffn_chunk(h2, cols, wg_ref, wu_ref, sample_act)
        ys_ref[:, 0, :] = _dense_close(x1, sample_act[...], wd_ref, fn_ref)
        for _ in mixer():
            pass

    @pl.when(step == n_tiles)
    def _():
        x1, h2 = _dense_open(xd_ref[0], mixed_ref[...], wout_ref, n2_ref)
        for cols in chunks:
            _ffn_chunk(h2, cols, wg_ref, wu_ref, act_ref)
        y_ref[0] = _dense_close(x1, act_ref[...], wd_ref, fn_ref)

    @pl.when(jnp.logical_and(step > 0, step < n_tiles))
    def _():
        update_sample_caches()
        x1, h2 = _dense_open(xd_ref[0], mixed_ref[...], wout_ref, n2_ref)
        assert len(chunks) == 6
        phases = mixer()
        proj_done = next(phases)
        _ffn_chunk(h2, chunks[0], wg_ref, wu_ref, act_ref)
        _ffn_chunk(h2, chunks[1], wg_ref, wu_ref, act_ref)
        _ffn_chunk(_after(h2, proj_done), chunks[2], wg_ref, wu_ref, act_ref)
        score_done = next(phases)
        _ffn_chunk(h2, chunks[3], wg_ref, wu_ref, act_ref)
        _ffn_chunk(_after(h2, score_done), chunks[4], wg_ref, wu_ref, act_ref)
        _ffn_chunk(h2, chunks[5], wg_ref, wu_ref, act_ref)
        exp_done = next(phases)
        act = _after(act_ref[...], exp_done)
        next(phases)
        y_ref[0] = _dense_close(x1, act, wd_ref, fn_ref)


def _mixer_phases(x_ref, n1_ref, win_ref, bias_ref, sink_ref, wpool_ref, pscale_ref,
                  kout_ref, vout_ref, pout_ref, kt_ref, kprev_ref, vv_ref, ubuf_ref, mixed_ref,
                  s_ref, m_ref, p_ref, t):
    T = x_ref.shape[1]
    nblk = T // BLOCK
    ROWS = 2 * BLOCK
    KEYS = 2 * BLOCK

    x = x_ref[0]
    h = _rms(x, n1_ref[...]).astype(BF16)
    proj = _dot(h, win_ref[...])
    yield [proj[T - SUBLANES:, proj.shape[1] - LANES:]]
    q = (proj[:, 0:ATTN_WIDTH] * ATTN_SCALE).astype(BF16)
    k = proj[:, ATTN_WIDTH:ATTN_WIDTH + KV_WIDTH]
    v = proj[:, ATTN_WIDTH + KV_WIDTH:ATTN_WIDTH + 2 * KV_WIDTH]
    u = proj[:, ATTN_WIDTH + 2 * KV_WIDTH:]

    ubuf_ref[HIST_PAD:, :] = u
    pout_ref[0] = ubuf_ref[HIST_PAD + T - POOL_HIST:HIST_PAD + T, :]

    k_t = k.T
    kt_ref[...] = k_t.astype(BF16)
    kout_ref[0] = k_t[:, T - WINDOW:]
    vout_ref[0] = v[T - WINDOW:, :].T
    for idx, part in enumerate(_split_heads_to_lane_halves(v)):
        vv_ref[idx, BLOCK:, :] = part

    first_tile = t == 0
    before_start = lax.broadcasted_iota(jnp.int32, (ROWS, 2 * KEYS), 1) % KEYS < BLOCK
    score_done = []
    no_keys = jnp.zeros((HEAD_DIM, KEYS), BF16)
    for j in range(nblk):
        rows = slice(j * BLOCK, (j + 1) * BLOCK)
        for g in range(N_KV_HEADS):
            q2 = jnp.concatenate([q[rows, (2 * g) * LANES:(2 * g + 1) * LANES],
                                  q[rows, (2 * g + 1) * LANES:(2 * g + 2) * LANES]], axis=0)
            dims = slice(g * HEAD_DIM, (g + 1) * HEAD_DIM)
            if j == 0:
                kg = jnp.concatenate([kprev_ref[dims, :], kt_ref[dims, 0:BLOCK]], axis=1)
            else:
                kg = kt_ref[dims, (j - 1) * BLOCK:(j + 1) * BLOCK]
            kmat = jnp.concatenate([jnp.concatenate([kg, no_keys], axis=1),
                                    jnp.concatenate([no_keys, kg], axis=1)], axis=0)
            bias = bias_ref[g]
            if j == 0:
                bias = jnp.where(jnp.logical_and(first_tile, before_start), NEG_INF, bias)
            sc = _dot(q2, kmat) + bias
            s_ref[j * N_KV_HEADS + g] = sc
            score_done.append(sc[ROWS - SUBLANES:, 2 * KEYS - LANES:])
    yield score_done

    top = lax.broadcasted_iota(jnp.int32, (ROWS, LANES), 0) < BLOCK
    sink_b = [[jnp.where(top, sink_ref[GQA_GROUP * g + c], sink_ref[GQA_GROUP * g + 2 + c]) for c in range(2)]
              for g in range(N_KV_HEADS)]
    n_units = nblk * N_KV_HEADS
    for un in range(n_units):
        for c in range(2):
            sh = s_ref[un, :, c * KEYS:(c + 1) * KEYS]
            m_ref[un, c] = jnp.maximum(jnp.max(sh, axis=-1, keepdims=True), sink_b[un % N_KV_HEADS][c])
    exp_done = []
    for un in range(n_units):
        for c in range(2):
            sh = s_ref[un, :, c * KEYS:(c + 1) * KEYS]
            m = m_ref[un, c]
            e = jnp.exp(sh - jnp.concatenate([m, m], axis=1))
            p_ref[un, :, c * KEYS:(c + 1) * KEYS] = e.astype(BF16)
            exp_done.append(e[ROWS - SUBLANES:, KEYS - LANES:])
    yield exp_done

    lo = lax.broadcasted_iota(jnp.int32, (ROWS, LANES), 1) < HEAD_DIM
    first_head_rows = lax.broadcasted_iota(jnp.int32, (2 * KEYS, LANES), 0) < KEYS
    first_head_cols = lax.broadcasted_iota(jnp.int32, (2 * KEYS, LANES), 1) < HEAD_DIM
    ones_cols = jnp.where(first_head_rows == first_head_cols, 1.0, 0.0).astype(BF16)
    for un in range(n_units):
        j, g = divmod(un, N_KV_HEADS)
        keys = slice(j * BLOCK, j * BLOCK + KEYS)
        vmat = jnp.concatenate([vv_ref[2 * g, keys, :], vv_ref[2 * g + 1, keys, :]], axis=0)
        o = _dot(p_ref[un], jnp.concatenate([vmat, ones_cols], axis=1))
        m_sel = jnp.where(lo, m_ref[un, 0], m_ref[un, 1])
        sink_sel = jnp.where(lo, sink_b[g][0], sink_b[g][1])
        out = o[:, 0:LANES] / (o[:, LANES:2 * LANES] + jnp.exp(sink_sel - m_sel))
        rows = slice(j * BLOCK, (j + 1) * BLOCK)
        mixed_ref[rows, (2 * g) * LANES:(2 * g + 1) * LANES] = out[0:BLOCK].astype(BF16)
        mixed_ref[rows, (2 * g + 1) * LANES:(2 * g + 2) * LANES] = out[BLOCK:ROWS].astype(BF16)

    pos = lax.broadcasted_iota(jnp.int32, (T, 1), 0) + t * T
    pooled = []
    for g, w in enumerate(POOL_WINDOWS):
        cols = slice(g * LANES, (g + 1) * LANES)
        cur = ubuf_ref[HIST_PAD:HIST_PAD + T, cols]
        win = cur
        for back in range(1, w):
            win = win + ubuf_ref[HIST_PAD - back:HIST_PAD - back + T, cols]
        count = jnp.minimum(w, pos + 1).astype(F32)
        pooled.append((win / count - cur).astype(BF16))
    for g2 in range(len(POOL_WINDOWS) // 2):
        cols = slice(2 * g2 * LANES, (2 * g2 + 2) * LANES)
        m2 = jnp.concatenate(pooled[2 * g2:2 * g2 + 2], axis=1)
        yg = _dot(m2, wpool_ref[g2]) * pscale_ref[:, cols]
        mixed_ref[:, ATTN_WIDTH + 2 * g2 * LANES:ATTN_WIDTH + (2 * g2 + 2) * LANES] = yg.astype(BF16)

    kprev_ref[...] = kt_ref[:, T - BLOCK:T]
    vv_ref[:, 0:BLOCK, :] = vv_ref[:, T:T + BLOCK, :]
    ubuf_ref[0:HIST_PAD, :] = ubuf_ref[T:T + HIST_PAD, :]
    yield None


def _sample_body(xs_ref, n1_ref, win_ref, kt_ref, vt_ref, hist_ref, bias_ref, sink_ref, wpool_ref, pscale_ref,
                 mix_ref, knewt_ref, vnewt_ref, npool_ref,
                 proj_ref, mixed_ref):
    SB = kt_ref.shape[0]
    L = kt_ref.shape[2]
    i = pl.program_id(0)

    @pl.when(i == 0)
    def _():
        h = _rms(xs_ref[:, 0, :], n1_ref[...]).astype(BF16)
        proj = _dot(h, win_ref[0].astype(BF16))
        proj_ref[...] = proj
        knewt_ref[...] = proj[:, ATTN_WIDTH:ATTN_WIDTH + KV_WIDTH].T
        vnewt_ref[...] = proj[:, ATTN_WIDTH + KV_WIDTH:ATTN_WIDTH + 2 * KV_WIDTH].T

    r0 = pl.multiple_of(i * SB, SB)
    pr = proj_ref[pl.ds(r0, SB), :]
    q = pr[:, 0:ATTN_WIDTH] * ATTN_SCALE
    knew = pr[:, ATTN_WIDTH:ATTN_WIDTH + KV_WIDTH]
    vnew = pr[:, ATTN_WIDTH + KV_WIDTH:ATTN_WIDTH + 2 * KV_WIDTH]
    u = pr[:, ATTN_WIDTH + 2 * KV_WIDTH:]

    npool_ref[0:POOL_HIST - 1] = hist_ref[1:POOL_HIST]
    npool_ref[POOL_HIST - 1] = u

    lo_blk = lax.broadcasted_iota(jnp.int32, (SB, LANES), 1) < HEAD_DIM
    zero_blk = jnp.zeros((SB, LANES), F32)
    qh = []
    for p in range(N_PAIRS):
        pair = q[:, p * LANES:(p + 1) * LANES]
        swapped = pltpu.roll(pair, HEAD_DIM, axis=1)
        if (2 * p) // GQA_GROUP == 0:
            qh += [jnp.where(lo_blk, pair, zero_blk), jnp.where(lo_blk, swapped, zero_blk)]
        else:
            qh += [jnp.where(lo_blk, zero_blk, swapped), jnp.where(lo_blk, zero_blk, pair)]

    rows64 = N_HEADS * SUBLANES
    sub = lax.broadcasted_iota(jnp.int32, (rows64, LANES), 0) % SUBLANES
    lo = lax.broadcasted_iota(jnp.int32, (SUBLANES, LANES), 1) < HEAD_DIM
    n_groups = SB // SUBLANES
    groups = [slice(r * SUBLANES, (r + 1) * SUBLANES) for r in range(n_groups)]
    lhs = jnp.concatenate([qh[h][grp] for grp in groups for h in range(N_HEADS)], axis=0)
    lhs_bf = lhs.astype(BF16)
    scores = []
    for r in range(n_groups):
        s = jnp.zeros((rows64, L), F32)
        for si in range(SUBLANES):
            kb = kt_ref[r * SUBLANES + si].astype(BF16)
            s = jnp.where(sub == si, _dot(lhs_bf[r * rows64:(r + 1) * rows64], kb), s)
        scores.append(s)
    s = jnp.concatenate(scores, axis=0) + jnp.concatenate([bias_ref[...]] * n_groups, axis=0)
    sink = jnp.concatenate([jnp.full((SUBLANES, 1), sink_ref[h], F32) for h in range(N_HEADS)] * n_groups, axis=0)
    knew_rep = jnp.concatenate([knew[grp] for grp in groups for _ in range(N_HEADS)], axis=0)
    vnew_rep = jnp.concatenate([vnew[grp] for grp in groups for _ in range(N_HEADS)], axis=0)
    s_new = jnp.sum(lhs * knew_rep, axis=-1, keepdims=True)
    m = jnp.maximum(jnp.maximum(jnp.max(s, axis=-1, keepdims=True), s_new), sink)
    e = jnp.exp(s - m)
    e_new = jnp.exp(s_new - m)
    denom = jnp.sum(e, axis=-1, keepdims=True) + e_new + jnp.exp(sink - m)
    e_bf = e.astype(BF16)
    outs = []
    for r in range(n_groups):
        o = jnp.zeros((rows64, LANES), F32)
        for si in range(SUBLANES):
            vb = vt_ref[r * SUBLANES + si].astype(BF16)
            o = jnp.where(sub == si, _dot_nt(e_bf[r * rows64:(r + 1) * rows64], vb), o)
        outs.append(o)
    o_all = (jnp.concatenate(outs, axis=0) + e_new * vnew_rep) / denom
    for r in range(n_groups):
        o = o_all[r * rows64:(r + 1) * rows64]
        out_rows = pl.ds(pl.multiple_of(r0 + r * SUBLANES, SUBLANES), SUBLANES)
        for p in range(N_PAIRS):
            a = o[2 * p * SUBLANES:(2 * p + 1) * SUBLANES]
            b = o[(2 * p + 1) * SUBLANES:(2 * p + 2) * SUBLANES]
            if (2 * p) // GQA_GROUP == 0:
                pair = jnp.where(lo, a, pltpu.roll(b, HEAD_DIM, axis=1))
            else:
                pair = jnp.where(lo, pltpu.roll(a, HEAD_DIM, axis=1), b)
            mixed_ref[out_rows, p * LANES:(p + 1) * LANES] = pair

    blk_rows = pl.ds(r0, SB)
    for g, w in enumerate(POOL_WINDOWS):
        cols = slice(g * LANES, (g + 1) * LANES)
        cur = u[:, cols]
        win = cur
        for back in range(1, w):
            win = win + hist_ref[POOL_HIST - back, :, cols]
        count = float(min(w, PAST_LEN + 1))
        m = (win / count - cur).astype(BF16)
        yg = _dot(m, wpool_ref[0, g].astype(BF16)) * pscale_ref[:, cols]
        mixed_ref[blk_rows, ATTN_WIDTH + g * LANES:ATTN_WIDTH + (g + 1) * LANES] = yg

    @pl.when(i == pl.num_programs(0) - 1)
    def _():
        mix_ref[...] = mixed_ref[...].astype(BF16)


def _alibi_slopes():
    return np.exp2(-8.0 * np.arange(1, N_HEADS + 1, dtype=np.float32) / N_HEADS).astype(np.float32)


def _prompt_bias():
    slopes = _alibi_slopes()
    qi = np.arange(BLOCK)[:, None]
    ki = np.arange(2 * BLOCK)[None, :]
    rel = BLOCK + qi - ki
    band = (rel >= 0) & (rel <= WINDOW)
    out = np.empty((N_KV_HEADS, 2 * BLOCK, 4 * BLOCK), np.float32)
    for g in range(N_KV_HEADS):
        for r in range(2):
            for c in range(2):
                slope = slopes[GQA_GROUP * g + 2 * r + c]
                val = np.where(band, -slope * rel.astype(np.float32), np.float32(NEG_INF))
                out[g, r * BLOCK:(r + 1) * BLOCK, c * 2 * BLOCK:(c + 1) * 2 * BLOCK] = val
    return out


def _sample_bias(L):
    assert PAST_LEN >= L and L <= WINDOW
    slopes = _alibi_slopes()
    rel = (L - np.arange(L)).astype(np.float32)
    return np.repeat(-slopes[:, None] * rel[None, :], SUBLANES, axis=0).astype(np.float32)


def _resident(shape):
    zeros = (0,) * len(shape)
    return pl.BlockSpec(shape, lambda *_: zeros, pipeline_mode=pl.Buffered(1))


def kernel(x_prompt, x_sample, cache_k_window, cache_v_window, state_pool, norm1, w_in, attn_sinks, w_pool,
           pool_scale, w_out, norm2, w_gate, w_up, w_down, final_norm):
    B, S, D = x_prompt.shape
    NS = x_sample.shape[0]
    L = cache_k_window.shape[2]
    d_ff = w_gate.shape[-1]
    pool_w = state_pool.shape[-1]
    assert norm1.shape[0] == 1 and x_sample.shape[1] == 1
    assert w_in.shape[-1] == ATTN_WIDTH + 2 * KV_WIDTH + pool_w and pool_w == len(POOL_WINDOWS) * LANES
    assert S % PROMPT_TILE == 0 and NS % SAMPLE_BLOCK == 0 and state_pool.shape[2] == POOL_HIST
    T = PROMPT_TILE

    assert D % (2 * BLOCK) == 0 and d_ff % (2 * BLOCK) == 0 and NS <= T
    in_w = w_in.shape[-1]
    n1 = norm1[0].reshape(1, D)
    n2 = norm2[0].reshape(1, D)
    fn = final_norm.reshape(1, D)
    pscale = pool_scale[0].reshape(1, pool_w)
    sinks = attn_sinks[0]
    xs = x_sample
    mixed_w = ATTN_WIDTH + pool_w

    def cache_to_kernel(c):
        return jnp.transpose(c[0], (0, 2, 3, 1)).reshape(NS, KV_WIDTH, L)

    def cache_from_kernel(c):
        n, _, keys = c.shape
        return jnp.transpose(c.reshape(n, N_KV_HEADS, HEAD_DIM, keys), (0, 3, 1, 2))[None]

    SB = SAMPLE_BLOCK
    kt_cache = cache_to_kernel(cache_k_window)
    vt_cache = cache_to_kernel(cache_v_window)
    mix_sample, knew_t, vnew_t, npool = pl.pallas_call(
        _sample_body,
        grid=(NS // SB,),
        in_specs=[_resident((NS, 1, D)),
                  _resident((1, D)),
                  _resident(w_in.shape),
                  pl.BlockSpec((SB, KV_WIDTH, L), lambda i: (i, 0, 0)),
                  pl.BlockSpec((SB, KV_WIDTH, L), lambda i: (i, 0, 0)),
                  pl.BlockSpec((POOL_HIST, SB, pool_w), lambda i: (0, i, 0)),
                  _resident((N_HEADS * SUBLANES, L)),
                  pl.BlockSpec(memory_space=pltpu.SMEM),
                  _resident(w_pool.shape),
                  _resident((1, pool_w))],
        out_specs=[pl.BlockSpec((NS, mixed_w), lambda i: (0, 0)),
                   pl.BlockSpec((KV_WIDTH, NS), lambda i: (0, 0)),
                   pl.BlockSpec((KV_WIDTH, NS), lambda i: (0, 0)),
                   pl.BlockSpec((POOL_HIST, SB, pool_w), lambda i: (0, i, 0))],
        out_shape=[jax.ShapeDtypeStruct((NS, mixed_w), BF16),
                   jax.ShapeDtypeStruct((KV_WIDTH, NS), F32),
                   jax.ShapeDtypeStruct((KV_WIDTH, NS), F32),
                   jax.ShapeDtypeStruct((POOL_HIST, NS, pool_w), F32)],
        scratch_shapes=[pltpu.VMEM((NS, in_w), F32),
                        pltpu.VMEM((NS, mixed_w), F32)],
        compiler_params=pltpu.CompilerParams(dimension_semantics=("arbitrary",),
                                             vmem_limit_bytes=VMEM_LIMIT_BYTES),
        name="sample_layer",
    )(xs, n1, w_in, kt_cache, vt_cache,
      jnp.transpose(state_pool[0], (1, 0, 2)), jnp.asarray(_sample_bias(L)), sinks, w_pool, pscale)

    n_units = (T // BLOCK) * N_KV_HEADS
    tiles_per_seq = S // T
    n_tiles = B * tiles_per_seq

    def mixer_tile(s):
        a = jnp.minimum(s, n_tiles - 1)
        return a // tiles_per_seq, a % tiles_per_seq

    def dense_tile(s):
        a = jnp.maximum(s - 1, 0)
        return a // tiles_per_seq, a % tiles_per_seq

    hbm = pl.BlockSpec(memory_space=pl.ANY)
    assert NS % n_tiles == 0 and NS == L == LANES
    cache_blk = pl.BlockSpec((NS // n_tiles, KV_WIDTH, L), lambda s: (jnp.minimum(s, n_tiles - 1), 0, 0))
    y_prompt, k_last, v_last, pool_last, y_sample, nkt, nvt = pl.pallas_call(
        functools.partial(_prompt_body, tiles_per_seq=tiles_per_seq, n_tiles=n_tiles),
        grid=(n_tiles + 1,),
        in_specs=[pl.BlockSpec((1, T, D), lambda s: (*mixer_tile(s), 0)),
                  pl.BlockSpec((1, T, D), lambda s: (*dense_tile(s), 0)),
                  _resident((NS, 1, D)),
                  _resident((NS, mixed_w)),
                  _resident((1, D)),
                  _resident((N_KV_HEADS, 2 * BLOCK, 4 * BLOCK)),
                  pl.BlockSpec(memory_space=pltpu.SMEM),
                  _resident((1, pool_w)),
                  _resident((1, D)),
                  _resident((1, D)),
                  cache_blk, cache_blk,
                  _resident((KV_WIDTH, NS)),
                  _resident((KV_WIDTH, NS)),
                  hbm, hbm, hbm, hbm, hbm, hbm],
        out_specs=[pl.BlockSpec((1, T, D), lambda s: (*dense_tile(s), 0)),
                   pl.BlockSpec((1, KV_WIDTH, WINDOW), lambda s: (mixer_tile(s)[0], 0, 0)),
                   pl.BlockSpec((1, KV_WIDTH, WINDOW), lambda s: (mixer_tile(s)[0], 0, 0)),
                   pl.BlockSpec((1, POOL_HIST, pool_w), lambda s: (mixer_tile(s)[0], 0, 0)),
                   pl.BlockSpec((NS, 1, D), lambda s: (0, 0, 0)),
                   cache_blk, cache_blk],
        out_shape=[jax.ShapeDtypeStruct((B, S, D), F32),
                   jax.ShapeDtypeStruct((B, KV_WIDTH, WINDOW), F32),
                   jax.ShapeDtypeStruct((B, KV_WIDTH, WINDOW), F32),
                   jax.ShapeDtypeStruct((B, POOL_HIST, pool_w), F32),
                   jax.ShapeDtypeStruct((NS, 1, D), F32),
                   jax.ShapeDtypeStruct((NS, KV_WIDTH, L), F32),
                   jax.ShapeDtypeStruct((NS, KV_WIDTH, L), F32)],
        scratch_shapes=[pltpu.VMEM((D, in_w), BF16),
                        pltpu.VMEM((len(POOL_WINDOWS) // 2, 2 * LANES, 2 * LANES), BF16),
                        pltpu.VMEM((mixed_w, D), BF16),
                        pltpu.VMEM((D, d_ff), BF16),
                        pltpu.VMEM((D, d_ff), BF16),
                        pltpu.VMEM((d_ff, D), BF16),
                        pltpu.VMEM(w_pool.shape[1:], F32),
                        pltpu.SemaphoreType.DMA((n_units + 1,)),
                        pltpu.VMEM((KV_WIDTH, T), BF16),
                        pltpu.VMEM((KV_WIDTH, BLOCK), BF16),
                        pltpu.VMEM((4, T + BLOCK, LANES), BF16),
                        pltpu.VMEM((T + HIST_PAD, pool_w), F32),
                        pltpu.VMEM((T, mixed_w), BF16),
                        pltpu.VMEM((T, d_ff), BF16),
                        pltpu.VMEM((n_units, 2 * BLOCK, 4 * BLOCK), F32),
                        pltpu.VMEM((n_units, 2, 2 * BLOCK, LANES), F32),
                        pltpu.VMEM((n_units, 2 * BLOCK, 4 * BLOCK), BF16)],
        compiler_params=pltpu.CompilerParams(dimension_semantics=("arbitrary",),
                                             vmem_limit_bytes=VMEM_LIMIT_BYTES),
        name="prompt_layer",
    )(x_prompt, x_prompt, xs, mix_sample, n1, jnp.asarray(_prompt_bias()), sinks, pscale, n2, fn,
      kt_cache, vt_cache, knew_t, vnew_t, w_in, w_pool, w_out, w_gate, w_up, w_down)

    return (y_prompt, y_sample, cache_from_kernel(k_last), cache_from_kernel(v_last),
            pool_last[None], cache_from_kernel(nkt), cache_from_kernel(nvt),
            jnp.transpose(npool, (1, 0, 2))[None])
```

```python
import functools

import numpy as np
import jax
import jax.numpy as jnp
from jax import lax
from jax.experimental import pallas as pl
from jax.experimental.pallas import tpu as pltpu

N_HEADS = 8
N_KV_HEADS = 2
HEAD_DIM = 64
GQA_GROUP = N_HEADS // N_KV_HEADS
WINDOW = 128
BLOCK = WINDOW
PAST_LEN = 8192
POOL_WINDOWS = (2, 4, 8, 16)
POOL_HIST = max(POOL_WINDOWS) - 1
RMS_EPS = 1e-5
NEG_INF = -1e30
ATTN_SCALE = HEAD_DIM ** -0.5

ATTN_WIDTH = N_HEADS * HEAD_DIM
KV_WIDTH = N_KV_HEADS * HEAD_DIM
N_PAIRS = N_HEADS // 2
LANES = 128
SUBLANES = 8
MXU_DEPTH = 256
HIST_PAD = 16

PROMPT_TILE = 512
SAMPLE_BLOCK = 32
FFN_CHUNK = 512
VMEM_LIMIT_BYTES = 62 * 1024 * 1024

F32 = jnp.float32
BF16 = jnp.bfloat16


def _rms(x, g):
    var = jnp.mean(x * x, axis=-1, keepdims=True)
    return x * lax.rsqrt(var + RMS_EPS) * g


def _dot(a, b):
    return jnp.dot(a, b, preferred_element_type=F32)


def _dot_nt(a, b):
    return lax.dot_general(a, b, (((1,), (1,)), ((), ())), preferred_element_type=F32)


def _ffn_chunks(d_ff):
    return [(c, min(c + FFN_CHUNK, d_ff)) for c in range(0, d_ff, FFN_CHUNK)]


def _dense_open(x, mixed_bf16, wout_ref, n2_ref):
    x1 = x + _dot(mixed_bf16, wout_ref[...])
    return x1, _rms(x1, n2_ref[...]).astype(BF16)


def _ffn_chunk(h2, cols, wg_ref, wu_ref, act_ref):
    gate = _dot(h2, wg_ref[:, cols])
    up = _dot(h2, wu_ref[:, cols])
    act_ref[:, cols] = (jax.nn.silu(gate) * up).astype(BF16)


def _dense_close(x1, act_bf16, wd_ref, fn_ref):
    return _rms(x1 + _dot(act_bf16, wd_ref[...]), fn_ref[...])


def _after(value, anchors):
    acc = anchors[0]
    for a in anchors[1:]:
        acc = acc + a
    bits = lax.bitcast_convert_type(acc, jnp.int32)
    zero = lax.shift_right_logical(lax.shift_right_logical(bits, 16), 16)[0:1, 0:1]
    head = value[:, 0:MXU_DEPTH] + zero.astype(value.dtype)
    return jnp.concatenate([head, value[:, MXU_DEPTH:]], axis=1)


def _split_heads_to_lane_halves(a):
    lo = lax.broadcasted_iota(jnp.int32, a.shape, 1) < HEAD_DIM
    zero = jnp.zeros_like(a)
    ar = pltpu.roll(a, HEAD_DIM, axis=1)
    return (jnp.where(lo, a, zero).astype(BF16), jnp.where(lo, zero, ar).astype(BF16),
            jnp.where(lo, ar, zero).astype(BF16), jnp.where(lo, zero, a).astype(BF16))


def _load_weights_as_bf16(weights, ring_ref, sem):
    n_slots, rows, max_cols = ring_ref.shape
    chunks = [(src, dst, r0, c0, min(max_cols, dst.shape[1] - c0))
              for src, dst in weights
              for r0 in range(0, dst.shape[0], rows)
              for c0 in range(0, dst.shape[1], max_cols)]

    def chunk_copy(i):
        src, _, r0, c0, cols = chunks[i]
        slot = i % n_slots
        return pltpu.make_async_copy(src.at[0, pl.ds(r0, rows), pl.ds(c0, cols)],
                                     ring_ref.at[slot, :, 0:cols], sem.at[slot])

    for i in range(min(n_slots - 1, len(chunks))):
        chunk_copy(i).start()
    for i, (_, dst, r0, c0, cols) in enumerate(chunks):
        nxt = i + n_slots - 1
        if nxt < len(chunks):
            chunk_copy(nxt).start()
        chunk_copy(i).wait()
        dst[r0:r0 + rows, c0:c0 + cols] = ring_ref[i % n_slots, :, 0:cols].astype(BF16)


def _prompt_body(x_ref, xd_ref, xs_ref, mixs_ref, n1_ref, bias_ref, sink_ref, pscale_ref, n2_ref, fn_ref,
                 kc_ref, vc_ref, knewt_ref, vnewt_ref,
                 win_hbm, wpool_hbm, wout_hbm, wg_hbm, wu_hbm, wd_hbm,
                 y_ref, kout_ref, vout_ref, pout_ref, ys_ref, nkc_ref, nvc_ref,
                 win_ref, wpool_ref, wout_ref, wg_ref, wu_ref, wd_ref, pool_stage_ref, dma_sem,
                 kt_ref, kprev_ref, vv_ref, ubuf_ref, mixed_ref, act_ref, s_ref, m_ref, p_ref,
                 *, tiles_per_seq, n_tiles):
    T = x_ref.shape[1]
    NS = xs_ref.shape[0]
    step = pl.program_id(0)
    t = jnp.minimum(step, n_tiles - 1) % tiles_per_seq
    nblk = T // BLOCK
    ROWS = 2 * BLOCK
    KEYS = 2 * BLOCK

    @pl.when(t == 0)
    def _():
        kprev_ref[...] = jnp.zeros(kprev_ref.shape, BF16)
        vv_ref[:, 0:BLOCK, :] = jnp.zeros((4, BLOCK, LANES), BF16)
        ubuf_ref[0:HIST_PAD, :] = jnp.zeros((HIST_PAD, ubuf_ref.shape[1]), F32)

    def update_sample_caches():
        CS, _, L = kc_ref.shape
        last_lane = lax.broadcasted_iota(jnp.int32, (KV_WIDTH, L), 1) == L - 1
        first_sample = step * CS
        for new_t_ref, cache_ref, out_ref in ((knewt_ref, kc_ref, nkc_ref), (vnewt_ref, vc_ref, nvc_ref)):
            new_t = new_t_ref[...]
            for i in range(CS):
                new_col = pltpu.roll(new_t, L - 1 - (first_sample + i), axis=1)
                out_ref[i] = jnp.where(last_lane, new_col, pltpu.roll(cache_ref[i], L - 1, axis=1))

    def mixer():
        return _mixer_phases(x_ref, n1_ref, win_ref, bias_ref, sink_ref, wpool_ref, pscale_ref,
                             kout_ref, vout_ref, pout_ref, kt_ref, kprev_ref, vv_ref, ubuf_ref, mixed_ref,
                             s_ref, m_ref, p_ref, t, jnp.minimum(step, n_tiles - 1) // tiles_per_seq)

    chunks = [slice(c0, c1) for c0, c1 in _ffn_chunks(wg_ref.shape[1])]

    @pl.when(step == 0)
    def _():
        pool_copy = pltpu.make_async_copy(wpool_hbm.at[0], pool_stage_ref, dma_sem.at[s_ref.shape[0]])
        pool_copy.start()
        _load_weights_as_bf16([(win_hbm, win_ref), (wout_hbm, wout_ref), (wg_hbm, wg_ref), (wu_hbm, wu_ref),
                               (wd_hbm, wd_ref)], s_ref, dma_sem)
        pool_copy.wait()
        zp = jnp.zeros((LANES, LANES), BF16)
        for i in range(wpool_ref.shape[0]):
            wa = pool_stage_ref[2 * i].astype(BF16)
            wb = pool_stage_ref[2 * i + 1].astype(BF16)
            wpool_ref[i] = jnp.concatenate([jnp.concatenate([wa, zp], axis=1),
                                            jnp.concatenate([zp, wb], axis=1)], axis=0)
        update_sample_caches()
        sample_act = act_ref.at[0:NS]
        x1, h2 = _dense_open(xs_ref[:, 0, :], mixs_ref[...], wout_ref, n2_ref)
        for cols in chunks:
            _ffn_chunk(h2, cols, wg_ref, wu_ref, sample_act)
        ys_ref[:, 0, :] = _dense_close(x1, sample_act[...], wd_ref, fn_ref)
        for _ in mixer():
            pass

    @pl.when(step == n_tiles)
    def _():
        x1, h2 = _dense_open(xd_ref[0], mixed_ref[...], wout_ref, n2_ref)
        for cols in chunks:
            _ffn_chunk(h2, cols, wg_ref, wu_ref, act_ref)
        y_ref[0] = _dense_close(x1, act_ref[...], wd_ref, fn_ref)

    @pl.when(jnp.logical_and(step > 0, step < n_tiles))
    def _():
        update_sample_caches()
        x1, h2 = _dense_open(xd_ref[0], mixed_ref[...], wout_ref, n2_ref)
        assert len(chunks) == 6
        phases = mixer()
        proj_done = next(phases)
        _ffn_chunk(h2, chunks[0], wg_ref, wu_ref, act_ref)
        _ffn_chunk(h2, chunks[1], wg_ref, wu_ref, act_ref)
        _ffn_chunk(_after(h2, proj_done), chunks[2], wg_ref, wu_ref, act_ref)
        score_done = next(phases)
        _ffn_chunk(h2, chunks[3], wg_ref, wu_ref, act_ref)
        _ffn_chunk(_after(h2, score_done), chunks[4], wg_ref, wu_ref, act_ref)
        _ffn_chunk(h2, chunks[5], wg_ref, wu_ref, act_ref)
        exp_done = next(phases)
        act = _after(act_ref[...], exp_done)
        next(phases)
        y_ref[0] = _dense_close(x1, act, wd_ref, fn_ref)


def _mixer_phases(x_ref, n1_ref, win_ref, bias_ref, sink_ref, wpool_ref, pscale_ref,
                  kout_ref, vout_ref, pout_ref, kt_ref, kprev_ref, vv_ref, ubuf_ref, mixed_ref,
                  s_ref, m_ref, p_ref, t, b):
    T = x_ref.shape[1]
    nblk = T // BLOCK
    ROWS = 2 * BLOCK
    KEYS = 2 * BLOCK

    x = x_ref[0]
    h = _rms(x, n1_ref[...]).astype(BF16)
    proj = _dot(h, win_ref[...])
    yield [proj[T - SUBLANES:, proj.shape[1] - LANES:]]
    q = (proj[:, 0:ATTN_WIDTH] * ATTN_SCALE).astype(BF16)
    k = proj[:, ATTN_WIDTH:ATTN_WIDTH + KV_WIDTH]
    v = proj[:, ATTN_WIDTH + KV_WIDTH:ATTN_WIDTH + 2 * KV_WIDTH]
    u = proj[:, ATTN_WIDTH + 2 * KV_WIDTH:]

    ubuf_ref[HIST_PAD:, :] = u
    pout_ref[:, pl.ds(b, 1), :] = ubuf_ref[HIST_PAD + T - POOL_HIST:HIST_PAD + T, :][:, None, :]

    k_t = k.T
    kt_ref[...] = k_t.astype(BF16)
    kout_ref[0] = k_t[:, T - WINDOW:]
    vout_ref[0] = v[T - WINDOW:, :].T
    for idx, part in enumerate(_split_heads_to_lane_halves(v)):
        vv_ref[idx, BLOCK:, :] = part

    first_tile = t == 0
    before_start = lax.broadcasted_iota(jnp.int32, (ROWS, 2 * KEYS), 1) % KEYS < BLOCK
    score_done = []
    no_keys = jnp.zeros((HEAD_DIM, KEYS), BF16)
    for j in range(nblk):
        rows = slice(j * BLOCK, (j + 1) * BLOCK)
        for g in range(N_KV_HEADS):
            q2 = jnp.concatenate([q[rows, (2 * g) * LANES:(2 * g + 1) * LANES],
                                  q[rows, (2 * g + 1) * LANES:(2 * g + 2) * LANES]], axis=0)
            dims = slice(g * HEAD_DIM, (g + 1) * HEAD_DIM)
            if j == 0:
                kg = jnp.concatenate([kprev_ref[dims, :], kt_ref[dims, 0:BLOCK]], axis=1)
            else:
                kg = kt_ref[dims, (j - 1) * BLOCK:(j + 1) * BLOCK]
            kmat = jnp.concatenate([jnp.concatenate([kg, no_keys], axis=1),
                                    jnp.concatenate([no_keys, kg], axis=1)], axis=0)
            bias = bias_ref[g]
            if j == 0:
                bias = jnp.where(jnp.logical_and(first_tile, before_start), NEG_INF, bias)
            sc = _dot(q2, kmat) + bias
            s_ref[j * N_KV_HEADS + g] = sc
            score_done.append(sc[ROWS - SUBLANES:, 2 * KEYS - LANES:])
    yield score_done

    top = lax.broadcasted_iota(jnp.int32, (ROWS, LANES), 0) < BLOCK
    sink_b = [[jnp.where(top, sink_ref[GQA_GROUP * g + c], sink_ref[GQA_GROUP * g + 2 + c]) for c in range(2)]
              for g in range(N_KV_HEADS)]
    n_units = nblk * N_KV_HEADS
    for un in range(n_units):
        for c in range(2):
            sh = s_ref[un, :, c * KEYS:(c + 1) * KEYS]
            m_ref[un, c] = jnp.maximum(jnp.max(sh, axis=-1, keepdims=True), sink_b[un % N_KV_HEADS][c])
    exp_done = []
    for un in range(n_units):
        for c in range(2):
            sh = s_ref[un, :, c * KEYS:(c + 1) * KEYS]
            m = m_ref[un, c]
            e = jnp.exp(sh - jnp.concatenate([m, m], axis=1))
            p_ref[un, :, c * KEYS:(c + 1) * KEYS] = e.astype(BF16)
            exp_done.append(e[ROWS - SUBLANES:, KEYS - LANES:])
    yield exp_done

    lo = lax.broadcasted_iota(jnp.int32, (ROWS, LANES), 1) < HEAD_DIM
    first_head_rows = lax.broadcasted_iota(jnp.int32, (2 * KEYS, LANES), 0) < KEYS
    first_head_cols = lax.broadcasted_iota(jnp.int32, (2 * KEYS, LANES), 1) < HEAD_DIM
    ones_cols = jnp.where(first_head_rows == first_head_cols, 1.0, 0.0).astype(BF16)
    for un in range(n_units):
        j, g = divmod(un, N_KV_HEADS)
        keys = slice(j * BLOCK, j * BLOCK + KEYS)
        vmat = jnp.concatenate([vv_ref[2 * g, keys, :], vv_ref[2 * g + 1, keys, :]], axis=0)
        o = _dot(p_ref[un], jnp.concatenate([vmat, ones_cols], axis=1))
        m_sel = jnp.where(lo, m_ref[un, 0], m_ref[un, 1])
        sink_sel = jnp.where(lo, sink_b[g][0], sink_b[g][1])
        out = o[:, 0:LANES] / (o[:, LANES:2 * LANES] + jnp.exp(sink_sel - m_sel))
        rows = slice(j * BLOCK, (j + 1) * BLOCK)
        mixed_ref[rows, (2 * g) * LANES:(2 * g + 1) * LANES] = out[0:BLOCK].astype(BF16)
        mixed_ref[rows, (2 * g + 1) * LANES:(2 * g + 2) * LANES] = out[BLOCK:ROWS].astype(BF16)

    pos = lax.broadcasted_iota(jnp.int32, (T, 1), 0) + t * T
    pooled = []
    for g, w in enumerate(POOL_WINDOWS):
        cols = slice(g * LANES, (g + 1) * LANES)
        cur = ubuf_ref[HIST_PAD:HIST_PAD + T, cols]
        win = cur
        for back in range(1, w):
            win = win + ubuf_ref[HIST_PAD - back:HIST_PAD - back + T, cols]
        count = jnp.minimum(w, pos + 1).astype(F32)
        pooled.append((win / count - cur).astype(BF16))
    for g2 in range(len(POOL_WINDOWS) // 2):
        cols = slice(2 * g2 * LANES, (2 * g2 + 2) * LANES)
        m2 = jnp.concatenate(pooled[2 * g2:2 * g2 + 2], axis=1)
        yg = _dot(m2, wpool_ref[g2]) * pscale_ref[:, cols]
        mixed_ref[:, ATTN_WIDTH + 2 * g2 * LANES:ATTN_WIDTH + (2 * g2 + 2) * LANES] = yg.astype(BF16)

    kprev_ref[...] = kt_ref[:, T - BLOCK:T]
    vv_ref[:, 0:BLOCK, :] = vv_ref[:, T:T + BLOCK, :]
    ubuf_ref[0:HIST_PAD, :] = ubuf_ref[T:T + HIST_PAD, :]
    yield None


def _sample_body(xs_ref, n1_ref, win_ref, kt_ref, vt_ref, hist_ref, bias_ref, sink_ref, wpool_ref, pscale_ref,
                 mix_ref, knewt_ref, vnewt_ref, npool_ref,
                 proj_ref, mixed_ref):
    SB = kt_ref.shape[0]
    L = kt_ref.shape[2]
    i = pl.program_id(0)

    @pl.when(i == 0)
    def _():
        h = _rms(xs_ref[:, 0, :], n1_ref[...]).astype(BF16)
        proj = _dot(h, win_ref[0].astype(BF16))
        proj_ref[...] = proj
        knewt_ref[...] = proj[:, ATTN_WIDTH:ATTN_WIDTH + KV_WIDTH].T
        vnewt_ref[...] = proj[:, ATTN_WIDTH + KV_WIDTH:ATTN_WIDTH + 2 * KV_WIDTH].T

    r0 = pl.multiple_of(i * SB, SB)
    pr = proj_ref[pl.ds(r0, SB), :]
    q = pr[:, 0:ATTN_WIDTH] * ATTN_SCALE
    knew = pr[:, ATTN_WIDTH:ATTN_WIDTH + KV_WIDTH]
    vnew = pr[:, ATTN_WIDTH + KV_WIDTH:ATTN_WIDTH + 2 * KV_WIDTH]
    u = pr[:, ATTN_WIDTH + 2 * KV_WIDTH:]

    npool_ref[0:POOL_HIST - 1] = hist_ref[1:POOL_HIST]
    npool_ref[POOL_HIST - 1] = u

    lo_blk = lax.broadcasted_iota(jnp.int32, (SB, LANES), 1) < HEAD_DIM
    zero_blk = jnp.zeros((SB, LANES), F32)
    qh = []
    for p in range(N_PAIRS):
        pair = q[:, p * LANES:(p + 1) * LANES]
        swapped = pltpu.roll(pair, HEAD_DIM, axis=1)
        if (2 * p) // GQA_GROUP == 0:
            qh += [jnp.where(lo_blk, pair, zero_blk), jnp.where(lo_blk, swapped, zero_blk)]
        else:
            qh += [jnp.where(lo_blk, zero_blk, swapped), jnp.where(lo_blk, zero_blk, pair)]

    rows64 = N_HEADS * SUBLANES
    sub = lax.broadcasted_iota(jnp.int32, (rows64, LANES), 0) % SUBLANES
    lo = lax.broadcasted_iota(jnp.int32, (SUBLANES, LANES), 1) < HEAD_DIM
    n_groups = SB // SUBLANES
    groups = [slice(r * SUBLANES, (r + 1) * SUBLANES) for r in range(n_groups)]
    lhs = jnp.concatenate([qh[h][grp] for grp in groups for h in range(N_HEADS)], axis=0)
    lhs_bf = lhs.astype(BF16)
    scores = []
    for r in range(n_groups):
        s = jnp.zeros((rows64, L), F32)
        for si in range(SUBLANES):
            kb = kt_ref[r * SUBLANES + si].astype(BF16)
            s = jnp.where(sub == si, _dot(lhs_bf[r * rows64:(r + 1) * rows64], kb), s)
        scores.append(s)
    s = jnp.concatenate(scores, axis=0) + jnp.concatenate([bias_ref[...]] * n_groups, axis=0)
    sink = jnp.concatenate([jnp.full((SUBLANES, 1), sink_ref[h], F32) for h in range(N_HEADS)] * n_groups, axis=0)
    knew_rep = jnp.concatenate([knew[grp] for grp in groups for _ in range(N_HEADS)], axis=0)
    vnew_rep = jnp.concatenate([vnew[grp] for grp in groups for _ in range(N_HEADS)], axis=0)
    s_new = jnp.sum(lhs * knew_rep, axis=-1, keepdims=True)
    m = jnp.maximum(jnp.maximum(jnp.max(s, axis=-1, keepdims=True), s_new), sink)
    e = jnp.exp(s - m)
    e_new = jnp.exp(s_new - m)
    denom = jnp.sum(e, axis=-1, keepdims=True) + e_new + jnp.exp(sink - m)
    e_bf = e.astype(BF16)
    outs = []
    for r in range(n_groups):
        o = jnp.zeros((rows64, LANES), F32)
        for si in range(SUBLANES):
            vb = vt_ref[r * SUBLANES + si].astype(BF16)
            o = jnp.where(sub == si, _dot_nt(e_bf[r * rows64:(r + 1) * rows64], vb), o)
        outs.append(o)
    o_all = (jnp.concatenate(outs, axis=0) + e_new * vnew_rep) / denom
    for r in range(n_groups):
        o = o_all[r * rows64:(r + 1) * rows64]
        out_rows = pl.ds(pl.multiple_of(r0 + r * SUBLANES, SUBLANES), SUBLANES)
        for p in range(N_PAIRS):
            a = o[2 * p * SUBLANES:(2 * p + 1) * SUBLANES]
            b = o[(2 * p + 1) * SUBLANES:(2 * p + 2) * SUBLANES]
            if (2 * p) // GQA_GROUP == 0:
                pair = jnp.where(lo, a, pltpu.roll(b, HEAD_DIM, axis=1))
            else:
                pair = jnp.where(lo, pltpu.roll(a, HEAD_DIM, axis=1), b)
            mixed_ref[out_rows, p * LANES:(p + 1) * LANES] = pair

    blk_rows = pl.ds(r0, SB)
    for g, w in enumerate(POOL_WINDOWS):
        cols = slice(g * LANES, (g + 1) * LANES)
        cur = u[:, cols]
        win = cur
        for back in range(1, w):
            win = win + hist_ref[POOL_HIST - back, :, cols]
        count = float(min(w, PAST_LEN + 1))
        m = (win / count - cur).astype(BF16)
        yg = _dot(m, wpool_ref[0, g].astype(BF16)) * pscale_ref[:, cols]
        mixed_ref[blk_rows, ATTN_WIDTH + g * LANES:ATTN_WIDTH + (g + 1) * LANES] = yg

    @pl.when(i == pl.num_programs(0) - 1)
    def _():
        mix_ref[...] = mixed_ref[...].astype(BF16)


def _alibi_slopes():
    return np.exp2(-8.0 * np.arange(1, N_HEADS + 1, dtype=np.float32) / N_HEADS).astype(np.float32)


def _prompt_bias():
    slopes = _alibi_slopes()
    qi = np.arange(BLOCK)[:, None]
    ki = np.arange(2 * BLOCK)[None, :]
    rel = BLOCK + qi - ki
    band = (rel >= 0) & (rel <= WINDOW)
    out = np.empty((N_KV_HEADS, 2 * BLOCK, 4 * BLOCK), np.float32)
    for g in range(N_KV_HEADS):
        for r in range(2):
            for c in range(2):
                slope = slopes[GQA_GROUP * g + 2 * r + c]
                val = np.where(band, -slope * rel.astype(np.float32), np.float32(NEG_INF))
                out[g, r * BLOCK:(r + 1) * BLOCK, c * 2 * BLOCK:(c + 1) * 2 * BLOCK] = val
    return out


def _sample_bias(L):
    assert PAST_LEN >= L and L <= WINDOW
    slopes = _alibi_slopes()
    rel = (L - np.arange(L)).astype(np.float32)
    return np.repeat(-slopes[:, None] * rel[None, :], SUBLANES, axis=0).astype(np.float32)


def _resident(shape):
    zeros = (0,) * len(shape)
    return pl.BlockSpec(shape, lambda *_: zeros, pipeline_mode=pl.Buffered(1))


def kernel(x_prompt, x_sample, cache_k_window, cache_v_window, state_pool, norm1, w_in, attn_sinks, w_pool,
           pool_scale, w_out, norm2, w_gate, w_up, w_down, final_norm):
    B, S, D = x_prompt.shape
    NS = x_sample.shape[0]
    L = cache_k_window.shape[2]
    d_ff = w_gate.shape[-1]
    pool_w = state_pool.shape[-1]
    assert norm1.shape[0] == 1 and x_sample.shape[1] == 1
    assert w_in.shape[-1] == ATTN_WIDTH + 2 * KV_WIDTH + pool_w and pool_w == len(POOL_WINDOWS) * LANES
    assert S % PROMPT_TILE == 0 and NS % SAMPLE_BLOCK == 0 and state_pool.shape[2] == POOL_HIST
    T = PROMPT_TILE

    assert D % (2 * BLOCK) == 0 and d_ff % (2 * BLOCK) == 0 and NS <= T
    in_w = w_in.shape[-1]
    n1 = norm1[0].reshape(1, D)
    n2 = norm2[0].reshape(1, D)
    fn = final_norm.reshape(1, D)
    pscale = pool_scale[0].reshape(1, pool_w)
    sinks = attn_sinks[0]
    xs = x_sample
    mixed_w = ATTN_WIDTH + pool_w

    def cache_to_kernel(c):
        return jnp.transpose(c[0], (0, 2, 3, 1)).reshape(NS, KV_WIDTH, L)

    def cache_from_kernel(c):
        n, _, keys = c.shape
        return jnp.transpose(c.reshape(n, N_KV_HEADS, HEAD_DIM, keys), (0, 3, 1, 2))[None]

    SB = SAMPLE_BLOCK
    kt_cache = cache_to_kernel(cache_k_window)
    vt_cache = cache_to_kernel(cache_v_window)
    mix_sample, knew_t, vnew_t, npool = pl.pallas_call(
        _sample_body,
        grid=(NS // SB,),
        in_specs=[_resident((NS, 1, D)),
                  _resident((1, D)),
                  _resident(w_in.shape),
                  pl.BlockSpec((SB, KV_WIDTH, L), lambda i: (i, 0, 0)),
                  pl.BlockSpec((SB, KV_WIDTH, L), lambda i: (i, 0, 0)),
                  pl.BlockSpec((POOL_HIST, SB, pool_w), lambda i: (0, i, 0)),
                  _resident((N_HEADS * SUBLANES, L)),
                  pl.BlockSpec(memory_space=pltpu.SMEM),
                  _resident(w_pool.shape),
                  _resident((1, pool_w))],
        out_specs=[pl.BlockSpec((NS, mixed_w), lambda i: (0, 0)),
                   pl.BlockSpec((KV_WIDTH, NS), lambda i: (0, 0)),
                   pl.BlockSpec((KV_WIDTH, NS), lambda i: (0, 0)),
                   pl.BlockSpec((POOL_HIST, SB, pool_w), lambda i: (0, i, 0))],
        out_shape=[jax.ShapeDtypeStruct((NS, mixed_w), BF16),
                   jax.ShapeDtypeStruct((KV_WIDTH, NS), F32),
                   jax.ShapeDtypeStruct((KV_WIDTH, NS), F32),
                   jax.ShapeDtypeStruct((POOL_HIST, NS, pool_w), F32)],
        scratch_shapes=[pltpu.VMEM((NS, in_w), F32),
                        pltpu.VMEM((NS, mixed_w), F32)],
        compiler_params=pltpu.CompilerParams(dimension_semantics=("arbitrary",),
                                             vmem_limit_bytes=VMEM_LIMIT_BYTES),
        name="sample_layer",
    )(xs, n1, w_in, kt_cache, vt_cache,
      jnp.transpose(state_pool[0], (1, 0, 2)), jnp.asarray(_sample_bias(L)), sinks, w_pool, pscale)

    n_units = (T // BLOCK) * N_KV_HEADS
    tiles_per_seq = S // T
    n_tiles = B * tiles_per_seq

    def mixer_tile(s):
        a = jnp.minimum(s, n_tiles - 1)
        return a // tiles_per_seq, a % tiles_per_seq

    def dense_tile(s):
        a = jnp.maximum(s - 1, 0)
        return a // tiles_per_seq, a % tiles_per_seq

    hbm = pl.BlockSpec(memory_space=pl.ANY)
    assert NS % n_tiles == 0 and NS == L == LANES
    cache_blk = pl.BlockSpec((NS // n_tiles, KV_WIDTH, L), lambda s: (jnp.minimum(s, n_tiles - 1), 0, 0))
    y_prompt, k_last, v_last, pool_last, y_sample, nkt, nvt = pl.pallas_call(
        functools.partial(_prompt_body, tiles_per_seq=tiles_per_seq, n_tiles=n_tiles),
        grid=(n_tiles + 1,),
        in_specs=[pl.BlockSpec((1, T, D), lambda s: (*mixer_tile(s), 0)),
                  pl.BlockSpec((1, T, D), lambda s: (*dense_tile(s), 0)),
                  _resident((NS, 1, D)),
                  _resident((NS, mixed_w)),
                  _resident((1, D)),
                  _resident((N_KV_HEADS, 2 * BLOCK, 4 * BLOCK)),
                  pl.BlockSpec(memory_space=pltpu.SMEM),
                  _resident((1, pool_w)),
                  _resident((1, D)),
                  _resident((1, D)),
                  cache_blk, cache_blk,
                  _resident((KV_WIDTH, NS)),
                  _resident((KV_WIDTH, NS)),
                  hbm, hbm, hbm, hbm, hbm, hbm],
        out_specs=[pl.BlockSpec((1, T, D), lambda s: (*dense_tile(s), 0)),
                   pl.BlockSpec((1, KV_WIDTH, WINDOW), lambda s: (mixer_tile(s)[0], 0, 0)),
                   pl.BlockSpec((1, KV_WIDTH, WINDOW), lambda s: (mixer_tile(s)[0], 0, 0)),
                   pl.BlockSpec((POOL_HIST, B, pool_w), lambda s: (0, 0, 0)),
                   pl.BlockSpec((NS, 1, D), lambda s: (0, 0, 0)),
                   cache_blk, cache_blk],
        out_shape=[jax.ShapeDtypeStruct((B, S, D), F32),
                   jax.ShapeDtypeStruct((B, KV_WIDTH, WINDOW), F32),
                   jax.ShapeDtypeStruct((B, KV_WIDTH, WINDOW), F32),
                   jax.ShapeDtypeStruct((POOL_HIST, B, pool_w), F32),
                   jax.ShapeDtypeStruct((NS, 1, D), F32),
                   jax.ShapeDtypeStruct((NS, KV_WIDTH, L), F32),
                   jax.ShapeDtypeStruct((NS, KV_WIDTH, L), F32)],
        scratch_shapes=[pltpu.VMEM((D, in_w), BF16),
                        pltpu.VMEM((len(POOL_WINDOWS) // 2, 2 * LANES, 2 * LANES), BF16),
                        pltpu.VMEM((mixed_w, D), BF16),
                        pltpu.VMEM((D, d_ff), BF16),
                        pltpu.VMEM((D, d_ff), BF16),
                        pltpu.VMEM((d_ff, D), BF16),
                        pltpu.VMEM(w_pool.shape[1:], F32),
                        pltpu.SemaphoreType.DMA((n_units + 1,)),
                        pltpu.VMEM((KV_WIDTH, T), BF16),
                        pltpu.VMEM((KV_WIDTH, BLOCK), BF16),
                        pltpu.VMEM((4, T + BLOCK, LANES), BF16),
                        pltpu.VMEM((T + HIST_PAD, pool_w), F32),
                        pltpu.VMEM((T, mixed_w), BF16),
                        pltpu.VMEM((T, d_ff), BF16),
                        pltpu.VMEM((n_units, 2 * BLOCK, 4 * BLOCK), F32),
                        pltpu.VMEM((n_units, 2, 2 * BLOCK, LANES), F32),
                        pltpu.VMEM((n_units, 2 * BLOCK, 4 * BLOCK), BF16)],
        compiler_params=pltpu.CompilerParams(dimension_semantics=("arbitrary",),
                                             vmem_limit_bytes=VMEM_LIMIT_BYTES),
        name="prompt_layer",
    )(x_prompt, x_prompt, xs, mix_sample, n1, jnp.asarray(_prompt_bias()), sinks, pscale, n2, fn,
      kt_cache, vt_cache, knew_t, vnew_t, w_in, w_pool, w_out, w_gate, w_up, w_down)

    return (y_prompt, y_sample, cache_from_kernel(k_last), cache_from_kernel(v_last),
            jnp.transpose(pool_last, (1, 0, 2))[None], cache_from_kernel(nkt), cache_from_kernel(nvt),
            jnp.transpose(npool, (1, 0, 2))[None])
```

```python
import functools

import numpy as np
import jax
import jax.numpy as jnp
from jax import lax
from jax.experimental import pallas as pl
from jax.experimental.pallas import tpu as pltpu

N_HEADS = 8
N_KV_HEADS = 2
HEAD_DIM = 64
GQA_GROUP = N_HEADS // N_KV_HEADS
WINDOW = 128
BLOCK = WINDOW
PAST_LEN = 8192
POOL_WINDOWS = (2, 4, 8, 16)
POOL_HIST = max(POOL_WINDOWS) - 1
RMS_EPS = 1e-5
NEG_INF = -1e30
ATTN_SCALE = HEAD_DIM ** -0.5

ATTN_WIDTH = N_HEADS * HEAD_DIM
KV_WIDTH = N_KV_HEADS * HEAD_DIM
N_PAIRS = N_HEADS // 2
LANES = 128
SUBLANES = 8
MXU_DEPTH = 256
HIST_PAD = 16

PROMPT_TILE = 512
SAMPLE_BLOCK = 32
FFN_CHUNK = 512
VMEM_LIMIT_BYTES = 62 * 1024 * 1024

F32 = jnp.float32
BF16 = jnp.bfloat16


def _rms(x, g):
    var = jnp.mean(x * x, axis=-1, keepdims=True)
    return x * lax.rsqrt(var + RMS_EPS) * g


def _dot(a, b):
    return jnp.dot(a, b, preferred_element_type=F32)


def _dot_nt(a, b):
    return lax.dot_general(a, b, (((1,), (1,)), ((), ())), preferred_element_type=F32)


def _ffn_chunks(d_ff):
    return [(c, min(c + FFN_CHUNK, d_ff)) for c in range(0, d_ff, FFN_CHUNK)]


def _dense_open(x, mixed_bf16, wout_ref, n2_ref, second_half_after=None):
    if second_half_after is None:
        x1 = x + _dot(mixed_bf16, wout_ref[...])
    else:
        half = x.shape[0] // 2
        x1 = jnp.concatenate([x[:half] + _dot(mixed_bf16[:half], wout_ref[...]),
                              x[half:] + _dot(_after(mixed_bf16[half:], second_half_after), wout_ref[...])], axis=0)
    return x1, _rms(x1, n2_ref[...]).astype(BF16)


def _ffn_chunk(h2, cols, wg_ref, wu_ref, act_ref):
    gate = _dot(h2, wg_ref[:, cols])
    up = _dot(h2, wu_ref[:, cols])
    act_ref[:, cols] = (jax.nn.silu(gate) * up).astype(BF16)


def _dense_close(x1, act_bf16, wd_ref, fn_ref):
    return _rms(x1 + _dot(act_bf16, wd_ref[...]), fn_ref[...])


def _after(value, anchors):
    acc = anchors[0]
    for a in anchors[1:]:
        acc = acc + a
    bits = lax.bitcast_convert_type(acc, jnp.int32)
    zero = lax.shift_right_logical(lax.shift_right_logical(bits, 16), 16)[0:1, 0:1]
    head = value[:, 0:MXU_DEPTH] + zero.astype(value.dtype)
    return jnp.concatenate([head, value[:, MXU_DEPTH:]], axis=1)


def _split_heads_to_lane_halves(a):
    lo = lax.broadcasted_iota(jnp.int32, a.shape, 1) < HEAD_DIM
    zero = jnp.zeros_like(a)
    ar = pltpu.roll(a, HEAD_DIM, axis=1)
    return (jnp.where(lo, a, zero).astype(BF16), jnp.where(lo, zero, ar).astype(BF16),
            jnp.where(lo, ar, zero).astype(BF16), jnp.where(lo, zero, a).astype(BF16))


def _load_weights_as_bf16(weights, ring_ref, sem):
    n_slots, rows, max_cols = ring_ref.shape
    chunks = [(src, dst, r0, c0, min(max_cols, dst.shape[1] - c0))
              for src, dst in weights
              for r0 in range(0, dst.shape[0], rows)
              for c0 in range(0, dst.shape[1], max_cols)]

    def chunk_copy(i):
        src, _, r0, c0, cols = chunks[i]
        slot = i % n_slots
        return pltpu.make_async_copy(src.at[0, pl.ds(r0, rows), pl.ds(c0, cols)],
                                     ring_ref.at[slot, :, 0:cols], sem.at[slot])

    for i in range(min(n_slots - 1, len(chunks))):
        chunk_copy(i).start()
    for i, (_, dst, r0, c0, cols) in enumerate(chunks):
        nxt = i + n_slots - 1
        if nxt < len(chunks):
            chunk_copy(nxt).start()
        chunk_copy(i).wait()
        dst[r0:r0 + rows, c0:c0 + cols] = ring_ref[i % n_slots, :, 0:cols].astype(BF16)


def _prompt_body(x_ref, xd_ref, xs_ref, mixs_ref, n1_ref, bias_ref, sink_ref, pscale_ref, n2_ref, fn_ref,
                 kc_ref, vc_ref, knewt_ref, vnewt_ref,
                 win_hbm, wpool_hbm, wout_hbm, wg_hbm, wu_hbm, wd_hbm,
                 y_ref, kout_ref, vout_ref, pout_ref, ys_ref, nkc_ref, nvc_ref,
                 win_ref, wpool_ref, wout_ref, wg_ref, wu_ref, wd_ref, pool_stage_ref, dma_sem,
                 kt_ref, kprev_ref, vv_ref, ubuf_ref, mixed_ref, act_ref, s_ref, m_ref, p_ref,
                 *, tiles_per_seq, n_tiles):
    T = x_ref.shape[1]
    NS = xs_ref.shape[0]
    step = pl.program_id(0)
    t = jnp.minimum(step, n_tiles - 1) % tiles_per_seq
    nblk = T // BLOCK
    ROWS = 2 * BLOCK
    KEYS = 2 * BLOCK

    @pl.when(t == 0)
    def _():
        kprev_ref[...] = jnp.zeros(kprev_ref.shape, BF16)
        vv_ref[:, 0:BLOCK, :] = jnp.zeros((4, BLOCK, LANES), BF16)
        ubuf_ref[0:HIST_PAD, :] = jnp.zeros((HIST_PAD, ubuf_ref.shape[1]), F32)

    def update_sample_caches():
        CS, _, L = kc_ref.shape
        last_lane = lax.broadcasted_iota(jnp.int32, (KV_WIDTH, L), 1) == L - 1
        first_sample = step * CS
        for new_t_ref, cache_ref, out_ref in ((knewt_ref, kc_ref, nkc_ref), (vnewt_ref, vc_ref, nvc_ref)):
            new_t = new_t_ref[...]
            for i in range(CS):
                new_col = pltpu.roll(new_t, L - 1 - (first_sample + i), axis=1)
                out_ref[i] = jnp.where(last_lane, new_col, pltpu.roll(cache_ref[i], L - 1, axis=1))

    def mixer():
        return _mixer_phases(x_ref, n1_ref, win_ref, bias_ref, sink_ref, wpool_ref, pscale_ref,
                             kout_ref, vout_ref, pout_ref, kt_ref, kprev_ref, vv_ref, ubuf_ref, mixed_ref,
                             s_ref, m_ref, p_ref, t, jnp.minimum(step, n_tiles - 1) // tiles_per_seq)

    chunks = [slice(c0, c1) for c0, c1 in _ffn_chunks(wg_ref.shape[1])]

    @pl.when(step == 0)
    def _():
        pool_copy = pltpu.make_async_copy(wpool_hbm.at[0], pool_stage_ref, dma_sem.at[s_ref.shape[0]])
        pool_copy.start()
        _load_weights_as_bf16([(win_hbm, win_ref), (wout_hbm, wout_ref), (wg_hbm, wg_ref), (wu_hbm, wu_ref),
                               (wd_hbm, wd_ref)], s_ref, dma_sem)
        pool_copy.wait()
        zp = jnp.zeros((LANES, LANES), BF16)
        for i in range(wpool_ref.shape[0]):
            wa = pool_stage_ref[2 * i].astype(BF16)
            wb = pool_stage_ref[2 * i + 1].astype(BF16)
            wpool_ref[i] = jnp.concatenate([jnp.concatenate([wa, zp], axis=1),
                                            jnp.concatenate([zp, wb], axis=1)], axis=0)
        update_sample_caches()
        sample_act = act_ref.at[0:NS]
        x1, h2 = _dense_open(xs_ref[:, 0, :], mixs_ref[...], wout_ref, n2_ref)
        for cols in chunks:
            _ffn_chunk(h2, cols, wg_ref, wu_ref, sample_act)
        ys_ref[:, 0, :] = _dense_close(x1, sample_act[...], wd_ref, fn_ref)
        for _ in mixer():
            pass

    @pl.when(step == n_tiles)
    def _():
        x1, h2 = _dense_open(xd_ref[0], mixed_ref[...], wout_ref, n2_ref)
        for cols in chunks:
            _ffn_chunk(h2, cols, wg_ref, wu_ref, act_ref)
        y_ref[0] = _dense_close(x1, act_ref[...], wd_ref, fn_ref)

    @pl.when(jnp.logical_and(step > 0, step < n_tiles))
    def _():
        update_sample_caches()
        assert len(chunks) == 6
        phases = mixer()
        norm_done = next(phases)
        x1, h2 = _dense_open(xd_ref[0], mixed_ref[...], wout_ref, n2_ref, second_half_after=norm_done)
        proj_done = next(phases)
        _ffn_chunk(h2, chunks[0], wg_ref, wu_ref, act_ref)
        _ffn_chunk(h2, chunks[1], wg_ref, wu_ref, act_ref)
        _ffn_chunk(_after(h2, proj_done), chunks[2], wg_ref, wu_ref, act_ref)
        score_done = next(phases)
        _ffn_chunk(h2, chunks[3], wg_ref, wu_ref, act_ref)
        _ffn_chunk(_after(h2, score_done), chunks[4], wg_ref, wu_ref, act_ref)
        _ffn_chunk(h2, chunks[5], wg_ref, wu_ref, act_ref)
        exp_done = next(phases)
        act = _after(act_ref[...], exp_done)
        next(phases)
        y_ref[0] = _dense_close(x1, act, wd_ref, fn_ref)


def _mixer_phases(x_ref, n1_ref, win_ref, bias_ref, sink_ref, wpool_ref, pscale_ref,
                  kout_ref, vout_ref, pout_ref, kt_ref, kprev_ref, vv_ref, ubuf_ref, mixed_ref,
                  s_ref, m_ref, p_ref, t, b):
    T = x_ref.shape[1]
    nblk = T // BLOCK
    ROWS = 2 * BLOCK
    KEYS = 2 * BLOCK

    x = x_ref[0]
    h = _rms(x, n1_ref[...])
    yield [h[T - SUBLANES:, h.shape[1] - LANES:]]
    proj = _dot(h.astype(BF16), win_ref[...])
    yield [proj[T - SUBLANES:, proj.shape[1] - LANES:]]
    q = (proj[:, 0:ATTN_WIDTH] * ATTN_SCALE).astype(BF16)
    k = proj[:, ATTN_WIDTH:ATTN_WIDTH + KV_WIDTH]
    v = proj[:, ATTN_WIDTH + KV_WIDTH:ATTN_WIDTH + 2 * KV_WIDTH]
    u = proj[:, ATTN_WIDTH + 2 * KV_WIDTH:]

    ubuf_ref[HIST_PAD:, :] = u
    pout_ref[:, pl.ds(b, 1), :] = ubuf_ref[HIST_PAD + T - POOL_HIST:HIST_PAD + T, :][:, None, :]

    k_t = k.T
    kt_ref[...] = k_t.astype(BF16)
    kout_ref[0] = k_t[:, T - WINDOW:]
    vout_ref[0] = v[T - WINDOW:, :].T
    for idx, part in enumerate(_split_heads_to_lane_halves(v)):
        vv_ref[idx, BLOCK:, :] = part

    first_tile = t == 0
    before_start = lax.broadcasted_iota(jnp.int32, (ROWS, 2 * KEYS), 1) % KEYS < BLOCK
    score_done = []
    no_keys = jnp.zeros((HEAD_DIM, KEYS), BF16)
    for j in range(nblk):
        rows = slice(j * BLOCK, (j + 1) * BLOCK)
        for g in range(N_KV_HEADS):
            q2 = jnp.concatenate([q[rows, (2 * g) * LANES:(2 * g + 1) * LANES],
                                  q[rows, (2 * g + 1) * LANES:(2 * g + 2) * LANES]], axis=0)
            dims = slice(g * HEAD_DIM, (g + 1) * HEAD_DIM)
            if j == 0:
                kg = jnp.concatenate([kprev_ref[dims, :], kt_ref[dims, 0:BLOCK]], axis=1)
            else:
                kg = kt_ref[dims, (j - 1) * BLOCK:(j + 1) * BLOCK]
            kmat = jnp.concatenate([jnp.concatenate([kg, no_keys], axis=1),
                                    jnp.concatenate([no_keys, kg], axis=1)], axis=0)
            bias = bias_ref[g]
            if j == 0:
                bias = jnp.where(jnp.logical_and(first_tile, before_start), NEG_INF, bias)
            sc = _dot(q2, kmat) + bias
            s_ref[j * N_KV_HEADS + g] = sc
            score_done.append(sc[ROWS - SUBLANES:, 2 * KEYS - LANES:])
    yield score_done

    top = lax.broadcasted_iota(jnp.int32, (ROWS, LANES), 0) < BLOCK
    sink_b = [[jnp.where(top, sink_ref[GQA_GROUP * g + c], sink_ref[GQA_GROUP * g + 2 + c]) for c in range(2)]
              for g in range(N_KV_HEADS)]
    n_units = nblk * N_KV_HEADS
    for un in range(n_units):
        for c in range(2):
            sh = s_ref[un, :, c * KEYS:(c + 1) * KEYS]
            m_ref[un, c] = jnp.maximum(jnp.max(sh, axis=-1, keepdims=True), sink_b[un % N_KV_HEADS][c])
    exp_done = []
    for un in range(n_units):
        for c in range(2):
            sh = s_ref[un, :, c * KEYS:(c + 1) * KEYS]
            m = m_ref[un, c]
            e = jnp.exp(sh - jnp.concatenate([m, m], axis=1))
            p_ref[un, :, c * KEYS:(c + 1) * KEYS] = e.astype(BF16)
            exp_done.append(e[ROWS - SUBLANES:, KEYS - LANES:])
    yield exp_done

    lo = lax.broadcasted_iota(jnp.int32, (ROWS, LANES), 1) < HEAD_DIM
    first_head_rows = lax.broadcasted_iota(jnp.int32, (2 * KEYS, LANES), 0) < KEYS
    first_head_cols = lax.broadcasted_iota(jnp.int32, (2 * KEYS, LANES), 1) < HEAD_DIM
    ones_cols = jnp.where(first_head_rows == first_head_cols, 1.0, 0.0).astype(BF16)
    for un in range(n_units):
        j, g = divmod(un, N_KV_HEADS)
        keys = slice(j * BLOCK, j * BLOCK + KEYS)
        vmat = jnp.concatenate([vv_ref[2 * g, keys, :], vv_ref[2 * g + 1, keys, :]], axis=0)
        o = _dot(p_ref[un], jnp.concatenate([vmat, ones_cols], axis=1))
        m_sel = jnp.where(lo, m_ref[un, 0], m_ref[un, 1])
        sink_sel = jnp.where(lo, sink_b[g][0], sink_b[g][1])
        out = o[:, 0:LANES] / (o[:, LANES:2 * LANES] + jnp.exp(sink_sel - m_sel))
        rows = slice(j * BLOCK, (j + 1) * BLOCK)
        mixed_ref[rows, (2 * g) * LANES:(2 * g + 1) * LANES] = out[0:BLOCK].astype(BF16)
        mixed_ref[rows, (2 * g + 1) * LANES:(2 * g + 2) * LANES] = out[BLOCK:ROWS].astype(BF16)

    pos = lax.broadcasted_iota(jnp.int32, (T, 1), 0) + t * T
    pooled = []
    for g, w in enumerate(POOL_WINDOWS):
        cols = slice(g * LANES, (g + 1) * LANES)
        cur = ubuf_ref[HIST_PAD:HIST_PAD + T, cols]
        win = cur
        for back in range(1, w):
            win = win + ubuf_ref[HIST_PAD - back:HIST_PAD - back + T, cols]
        count = jnp.minimum(w, pos + 1).astype(F32)
        pooled.append((win / count - cur).astype(BF16))
    for g2 in range(len(POOL_WINDOWS) // 2):
        cols = slice(2 * g2 * LANES, (2 * g2 + 2) * LANES)
        m2 = jnp.concatenate(pooled[2 * g2:2 * g2 + 2], axis=1)
        yg = _dot(m2, wpool_ref[g2]) * pscale_ref[:, cols]
        mixed_ref[:, ATTN_WIDTH + 2 * g2 * LANES:ATTN_WIDTH + (2 * g2 + 2) * LANES] = yg.astype(BF16)

    kprev_ref[...] = kt_ref[:, T - BLOCK:T]
    vv_ref[:, 0:BLOCK, :] = vv_ref[:, T:T + BLOCK, :]
    ubuf_ref[0:HIST_PAD, :] = ubuf_ref[T:T + HIST_PAD, :]
    yield None


def _sample_body(xs_ref, n1_ref, win_ref, kt_ref, vt_ref, hist_ref, bias_ref, sink_ref, wpool_ref, pscale_ref,
                 mix_ref, knewt_ref, vnewt_ref, npool_ref,
                 proj_ref, mixed_ref):
    SB = kt_ref.shape[0]
    L = kt_ref.shape[2]
    i = pl.program_id(0)

    @pl.when(i == 0)
    def _():
        h = _rms(xs_ref[:, 0, :], n1_ref[...]).astype(BF16)
        proj = _dot(h, win_ref[0].astype(BF16))
        proj_ref[...] = proj
        knewt_ref[...] = proj[:, ATTN_WIDTH:ATTN_WIDTH + KV_WIDTH].T
        vnewt_ref[...] = proj[:, ATTN_WIDTH + KV_WIDTH:ATTN_WIDTH + 2 * KV_WIDTH].T

    r0 = pl.multiple_of(i * SB, SB)
    pr = proj_ref[pl.ds(r0, SB), :]
    q = pr[:, 0:ATTN_WIDTH] * ATTN_SCALE
    knew = pr[:, ATTN_WIDTH:ATTN_WIDTH + KV_WIDTH]
    vnew = pr[:, ATTN_WIDTH + KV_WIDTH:ATTN_WIDTH + 2 * KV_WIDTH]
    u = pr[:, ATTN_WIDTH + 2 * KV_WIDTH:]

    npool_ref[0:POOL_HIST - 1] = hist_ref[1:POOL_HIST]
    npool_ref[POOL_HIST - 1] = u

    lo_blk = lax.broadcasted_iota(jnp.int32, (SB, LANES), 1) < HEAD_DIM
    zero_blk = jnp.zeros((SB, LANES), F32)
    qh = []
    for p in range(N_PAIRS):
        pair = q[:, p * LANES:(p + 1) * LANES]
        swapped = pltpu.roll(pair, HEAD_DIM, axis=1)
        if (2 * p) // GQA_GROUP == 0:
            qh += [jnp.where(lo_blk, pair, zero_blk), jnp.where(lo_blk, swapped, zero_blk)]
        else:
            qh += [jnp.where(lo_blk, zero_blk, swapped), jnp.where(lo_blk, zero_blk, pair)]

    rows64 = N_HEADS * SUBLANES
    sub = lax.broadcasted_iota(jnp.int32, (rows64, LANES), 0) % SUBLANES
    lo = lax.broadcasted_iota(jnp.int32, (SUBLANES, LANES), 1) < HEAD_DIM
    n_groups = SB // SUBLANES
    groups = [slice(r * SUBLANES, (r + 1) * SUBLANES) for r in range(n_groups)]
    lhs = jnp.concatenate([qh[h][grp] for grp in groups for h in range(N_HEADS)], axis=0)
    lhs_bf = lhs.astype(BF16)
    scores = []
    for r in range(n_groups):
        s = jnp.zeros((rows64, L), F32)
        for si in range(SUBLANES):
            kb = kt_ref[r * SUBLANES + si].astype(BF16)
            s = jnp.where(sub == si, _dot(lhs_bf[r * rows64:(r + 1) * rows64], kb), s)
        scores.append(s)
    s = jnp.concatenate(scores, axis=0) + jnp.concatenate([bias_ref[...]] * n_groups, axis=0)
    sink = jnp.concatenate([jnp.full((SUBLANES, 1), sink_ref[h], F32) for h in range(N_HEADS)] * n_groups, axis=0)
    knew_rep = jnp.concatenate([knew[grp] for grp in groups for _ in range(N_HEADS)], axis=0)
    vnew_rep = jnp.concatenate([vnew[grp] for grp in groups for _ in range(N_HEADS)], axis=0)
    s_new = jnp.sum(lhs * knew_rep, axis=-1, keepdims=True)
    m = jnp.maximum(jnp.maximum(jnp.max(s, axis=-1, keepdims=True), s_new), sink)
    e = jnp.exp(s - m)
    e_new = jnp.exp(s_new - m)
    denom = jnp.sum(e, axis=-1, keepdims=True) + e_new + jnp.exp(sink - m)
    e_bf = e.astype(BF16)
    outs = []
    for r in range(n_groups):
        o = jnp.zeros((rows64, LANES), F32)
        for si in range(SUBLANES):
            vb = vt_ref[r * SUBLANES + si].astype(BF16)
            o = jnp.where(sub == si, _dot_nt(e_bf[r * rows64:(r + 1) * rows64], vb), o)
        outs.append(o)
    o_all = (jnp.concatenate(outs, axis=0) + e_new * vnew_rep) / denom
    for r in range(n_groups):
        o = o_all[r * rows64:(r + 1) * rows64]
        out_rows = pl.ds(pl.multiple_of(r0 + r * SUBLANES, SUBLANES), SUBLANES)
        for p in range(N_PAIRS):
            a = o[2 * p * SUBLANES:(2 * p + 1) * SUBLANES]
            b = o[(2 * p + 1) * SUBLANES:(2 * p + 2) * SUBLANES]
            if (2 * p) // GQA_GROUP == 0:
                pair = jnp.where(lo, a, pltpu.roll(b, HEAD_DIM, axis=1))
            else:
                pair = jnp.where(lo, pltpu.roll(a, HEAD_DIM, axis=1), b)
            mixed_ref[out_rows, p * LANES:(p + 1) * LANES] = pair

    blk_rows = pl.ds(r0, SB)
    for g, w in enumerate(POOL_WINDOWS):
        cols = slice(g * LANES, (g + 1) * LANES)
        cur = u[:, cols]
        win = cur
        for back in range(1, w):
            win = win + hist_ref[POOL_HIST - back, :, cols]
        count = float(min(w, PAST_LEN + 1))
        m = (win / count - cur).astype(BF16)
        yg = _dot(m, wpool_ref[0, g].astype(BF16)) * pscale_ref[:, cols]
        mixed_ref[blk_rows, ATTN_WIDTH + g * LANES:ATTN_WIDTH + (g + 1) * LANES] = yg

    @pl.when(i == pl.num_programs(0) - 1)
    def _():
        mix_ref[...] = mixed_ref[...].astype(BF16)


def _alibi_slopes():
    return np.exp2(-8.0 * np.arange(1, N_HEADS + 1, dtype=np.float32) / N_HEADS).astype(np.float32)


def _prompt_bias():
    slopes = _alibi_slopes()
    qi = np.arange(BLOCK)[:, None]
    ki = np.arange(2 * BLOCK)[None, :]
    rel = BLOCK + qi - ki
    band = (rel >= 0) & (rel <= WINDOW)
    out = np.empty((N_KV_HEADS, 2 * BLOCK, 4 * BLOCK), np.float32)
    for g in range(N_KV_HEADS):
        for r in range(2):
            for c in range(2):
                slope = slopes[GQA_GROUP * g + 2 * r + c]
                val = np.where(band, -slope * rel.astype(np.float32), np.float32(NEG_INF))
                out[g, r * BLOCK:(r + 1) * BLOCK, c * 2 * BLOCK:(c + 1) * 2 * BLOCK] = val
    return out


def _sample_bias(L):
    assert PAST_LEN >= L and L <= WINDOW
    slopes = _alibi_slopes()
    rel = (L - np.arange(L)).astype(np.float32)
    return np.repeat(-slopes[:, None] * rel[None, :], SUBLANES, axis=0).astype(np.float32)


def _resident(shape):
    zeros = (0,) * len(shape)
    return pl.BlockSpec(shape, lambda *_: zeros, pipeline_mode=pl.Buffered(1))


def kernel(x_prompt, x_sample, cache_k_window, cache_v_window, state_pool, norm1, w_in, attn_sinks, w_pool,
           pool_scale, w_out, norm2, w_gate, w_up, w_down, final_norm):
    B, S, D = x_prompt.shape
    NS = x_sample.shape[0]
    L = cache_k_window.shape[2]
    d_ff = w_gate.shape[-1]
    pool_w = state_pool.shape[-1]
    assert norm1.shape[0] == 1 and x_sample.shape[1] == 1
    assert w_in.shape[-1] == ATTN_WIDTH + 2 * KV_WIDTH + pool_w and pool_w == len(POOL_WINDOWS) * LANES
    assert S % PROMPT_TILE == 0 and NS % SAMPLE_BLOCK == 0 and state_pool.shape[2] == POOL_HIST
    T = PROMPT_TILE

    assert D % (2 * BLOCK) == 0 and d_ff % (2 * BLOCK) == 0 and NS <= T
    in_w = w_in.shape[-1]
    n1 = norm1[0].reshape(1, D)
    n2 = norm2[0].reshape(1, D)
    fn = final_norm.reshape(1, D)
    pscale = pool_scale[0].reshape(1, pool_w)
    sinks = attn_sinks[0]
    xs = x_sample
    mixed_w = ATTN_WIDTH + pool_w

    def cache_to_kernel(c):
        return jnp.transpose(c[0], (0, 2, 3, 1)).reshape(NS, KV_WIDTH, L)

    def cache_from_kernel(c):
        n, _, keys = c.shape
        return jnp.transpose(c.reshape(n, N_KV_HEADS, HEAD_DIM, keys), (0, 3, 1, 2))[None]

    SB = SAMPLE_BLOCK
    kt_cache = cache_to_kernel(cache_k_window)
    vt_cache = cache_to_kernel(cache_v_window)
    mix_sample, knew_t, vnew_t, npool = pl.pallas_call(
        _sample_body,
        grid=(NS // SB,),
        in_specs=[_resident((NS, 1, D)),
                  _resident((1, D)),
                  _resident(w_in.shape),
                  pl.BlockSpec((SB, KV_WIDTH, L), lambda i: (i, 0, 0)),
                  pl.BlockSpec((SB, KV_WIDTH, L), lambda i: (i, 0, 0)),
                  pl.BlockSpec((POOL_HIST, SB, pool_w), lambda i: (0, i, 0)),
                  _resident((N_HEADS * SUBLANES, L)),
                  pl.BlockSpec(memory_space=pltpu.SMEM),
                  _resident(w_pool.shape),
                  _resident((1, pool_w))],
        out_specs=[pl.BlockSpec((NS, mixed_w), lambda i: (0, 0)),
                   pl.BlockSpec((KV_WIDTH, NS), lambda i: (0, 0)),
                   pl.BlockSpec((KV_WIDTH, NS), lambda i: (0, 0)),
                   pl.BlockSpec((POOL_HIST, SB, pool_w), lambda i: (0, i, 0))],
        out_shape=[jax.ShapeDtypeStruct((NS, mixed_w), BF16),
                   jax.ShapeDtypeStruct((KV_WIDTH, NS), F32),
                   jax.ShapeDtypeStruct((KV_WIDTH, NS), F32),
                   jax.ShapeDtypeStruct((POOL_HIST, NS, pool_w), F32)],
        scratch_shapes=[pltpu.VMEM((NS, in_w), F32),
                        pltpu.VMEM((NS, mixed_w), F32)],
        compiler_params=pltpu.CompilerParams(dimension_semantics=("arbitrary",),
                                             vmem_limit_bytes=VMEM_LIMIT_BYTES),
        name="sample_layer",
    )(xs, n1, w_in, kt_cache, vt_cache,
      jnp.transpose(state_pool[0], (1, 0, 2)), jnp.asarray(_sample_bias(L)), sinks, w_pool, pscale)

    n_units = (T // BLOCK) * N_KV_HEADS
    tiles_per_seq = S // T
    n_tiles = B * tiles_per_seq

    def mixer_tile(s):
        a = jnp.minimum(s, n_tiles - 1)
        return a // tiles_per_seq, a % tiles_per_seq

    def dense_tile(s):
        a = jnp.maximum(s - 1, 0)
        return a // tiles_per_seq, a % tiles_per_seq

    hbm = pl.BlockSpec(memory_space=pl.ANY)
    assert NS % n_tiles == 0 and NS == L == LANES
    cache_blk = pl.BlockSpec((NS // n_tiles, KV_WIDTH, L), lambda s: (jnp.minimum(s, n_tiles - 1), 0, 0))
    y_prompt, k_last, v_last, pool_last, y_sample, nkt, nvt = pl.pallas_call(
        functools.partial(_prompt_body, tiles_per_seq=tiles_per_seq, n_tiles=n_tiles),
        grid=(n_tiles + 1,),
        in_specs=[pl.BlockSpec((1, T, D), lambda s: (*mixer_tile(s), 0)),
                  pl.BlockSpec((1, T, D), lambda s: (*dense_tile(s), 0)),
                  _resident((NS, 1, D)),
                  _resident((NS, mixed_w)),
                  _resident((1, D)),
                  _resident((N_KV_HEADS, 2 * BLOCK, 4 * BLOCK)),
                  pl.BlockSpec(memory_space=pltpu.SMEM),
                  _resident((1, pool_w)),
                  _resident((1, D)),
                  _resident((1, D)),
                  cache_blk, cache_blk,
                  _resident((KV_WIDTH, NS)),
                  _resident((KV_WIDTH, NS)),
                  hbm, hbm, hbm, hbm, hbm, hbm],
        out_specs=[pl.BlockSpec((1, T, D), lambda s: (*dense_tile(s), 0)),
                   pl.BlockSpec((1, KV_WIDTH, WINDOW), lambda s: (mixer_tile(s)[0], 0, 0)),
                   pl.BlockSpec((1, KV_WIDTH, WINDOW), lambda s: (mixer_tile(s)[0], 0, 0)),
                   pl.BlockSpec((POOL_HIST, B, pool_w), lambda s: (0, 0, 0)),
                   pl.BlockSpec((NS, 1, D), lambda s: (0, 0, 0)),
                   cache_blk, cache_blk],
        out_shape=[jax.ShapeDtypeStruct((B, S, D), F32),
                   jax.ShapeDtypeStruct((B, KV_WIDTH, WINDOW), F32),
                   jax.ShapeDtypeStruct((B, KV_WIDTH, WINDOW), F32),
                   jax.ShapeDtypeStruct((POOL_HIST, B, pool_w), F32),
                   jax.ShapeDtypeStruct((NS, 1, D), F32),
                   jax.ShapeDtypeStruct((NS, KV_WIDTH, L), F32),
                   jax.ShapeDtypeStruct((NS, KV_WIDTH, L), F32)],
        scratch_shapes=[pltpu.VMEM((D, in_w), BF16),
                        pltpu.VMEM((len(POOL_WINDOWS) // 2, 2 * LANES, 2 * LANES), BF16),
                        pltpu.VMEM((mixed_w, D), BF16),
                        pltpu.VMEM((D, d_ff), BF16),
                        pltpu.VMEM((D, d_ff), BF16),
                        pltpu.VMEM((d_ff, D), BF16),
                        pltpu.VMEM(w_pool.shape[1:], F32),
                        pltpu.SemaphoreType.DMA((n_units + 1,)),
                        pltpu.VMEM((KV_WIDTH, T), BF16),
                        pltpu.VMEM((KV_WIDTH, BLOCK), BF16),
                        pltpu.VMEM((4, T + BLOCK, LANES), BF16),
                        pltpu.VMEM((T + HIST_PAD, pool_w), F32),
                        pltpu.VMEM((T, mixed_w), BF16),
                        pltpu.VMEM((T, d_ff), BF16),
                        pltpu.VMEM((n_units, 2 * BLOCK, 4 * BLOCK), F32),
                        pltpu.VMEM((n_units, 2, 2 * BLOCK, LANES), F32),
                        pltpu.VMEM((n_units, 2 * BLOCK, 4 * BLOCK), BF16)],
        compiler_params=pltpu.CompilerParams(dimension_semantics=("arbitrary",),
                                             vmem_limit_bytes=VMEM_LIMIT_BYTES),
        name="prompt_layer",
    )(x_prompt, x_prompt, xs, mix_sample, n1, jnp.asarray(_prompt_bias()), sinks, pscale, n2, fn,
      kt_cache, vt_cache, knew_t, vnew_t, w_in, w_pool, w_out, w_gate, w_up, w_down)

    return (y_prompt, y_sample, cache_from_kernel(k_last), cache_from_kernel(v_last),
            jnp.transpose(pool_last, (1, 0, 2))[None], cache_from_kernel(nkt), cache_from_kernel(nvt),
            jnp.transpose(npool, (1, 0, 2))[None])
```

```python
import functools

import numpy as np
import jax
import jax.numpy as jnp
from jax import lax
from jax.experimental import pallas as pl
from jax.experimental.pallas import tpu as pltpu

N_HEADS = 8
N_KV_HEADS = 2
HEAD_DIM = 64
GQA_GROUP = N_HEADS // N_KV_HEADS
WINDOW = 128
BLOCK = WINDOW
PAST_LEN = 8192
POOL_WINDOWS = (2, 4, 8, 16)
POOL_HIST = max(POOL_WINDOWS) - 1
RMS_EPS = 1e-5
NEG_INF = -1e30
ATTN_SCALE = HEAD_DIM ** -0.5

ATTN_WIDTH = N_HEADS * HEAD_DIM
KV_WIDTH = N_KV_HEADS * HEAD_DIM
N_PAIRS = N_HEADS // 2
LANES = 128
SUBLANES = 8
MXU_DEPTH = 256
HIST_PAD = 16

PROMPT_TILE = 512
SAMPLE_BLOCK = 32
FFN_CHUNK = 512
VMEM_LIMIT_BYTES = 62 * 1024 * 1024

F32 = jnp.float32
BF16 = jnp.bfloat16


def _rms(x, g):
    var = jnp.mean(x * x, axis=-1, keepdims=True)
    return x * lax.rsqrt(var + RMS_EPS) * g


def _dot(a, b):
    return jnp.dot(a, b, preferred_element_type=F32)


def _dot_nt(a, b):
    return lax.dot_general(a, b, (((1,), (1,)), ((), ())), preferred_element_type=F32)


def _ffn_chunks(d_ff):
    return [(c, min(c + FFN_CHUNK, d_ff)) for c in range(0, d_ff, FFN_CHUNK)]


def _dense_open(x, mixed_bf16, wout_ref, n2_ref):
    x1 = x + _dot(mixed_bf16, wout_ref[...])
    return x1, _rms(x1, n2_ref[...]).astype(BF16)


def _ffn_chunk(h2, cols, wg_ref, wu_ref, act_ref):
    gate = _dot(h2, wg_ref[:, cols])
    up = _dot(h2, wu_ref[:, cols])
    act_ref[:, cols] = (jax.nn.silu(gate) * up).astype(BF16)


def _dense_close(x1, act_bf16, wd_ref, fn_ref):
    half = x1.shape[0] // 2
    if half % SUBLANES:
        return _rms(x1 + _dot(act_bf16, wd_ref[...]), fn_ref[...])
    return jnp.concatenate([_rms(x1[rows] + _dot(act_bf16[rows], wd_ref[...]), fn_ref[...])
                            for rows in (slice(0, half), slice(half, 2 * half))], axis=0)


def _after(value, anchors):
    acc = anchors[0]
    for a in anchors[1:]:
        acc = acc + a
    bits = lax.bitcast_convert_type(acc, jnp.int32)
    zero = lax.shift_right_logical(lax.shift_right_logical(bits, 16), 16)[0:1, 0:1]
    head = value[:, 0:MXU_DEPTH] + zero.astype(value.dtype)
    return jnp.concatenate([head, value[:, MXU_DEPTH:]], axis=1)


def _split_heads_to_lane_halves(a):
    lo = lax.broadcasted_iota(jnp.int32, a.shape, 1) < HEAD_DIM
    zero = jnp.zeros_like(a)
    ar = pltpu.roll(a, HEAD_DIM, axis=1)
    return (jnp.where(lo, a, zero).astype(BF16), jnp.where(lo, zero, ar).astype(BF16),
            jnp.where(lo, ar, zero).astype(BF16), jnp.where(lo, zero, a).astype(BF16))


def _load_weights_as_bf16(weights, ring_ref, sem):
    n_slots, rows, max_cols = ring_ref.shape
    chunks = [(src, dst, r0, c0, min(max_cols, dst.shape[1] - c0))
              for src, dst in weights
              for r0 in range(0, dst.shape[0], rows)
              for c0 in range(0, dst.shape[1], max_cols)]

    def chunk_copy(i):
        src, _, r0, c0, cols = chunks[i]
        slot = i % n_slots
        return pltpu.make_async_copy(src.at[0, pl.ds(r0, rows), pl.ds(c0, cols)],
                                     ring_ref.at[slot, :, 0:cols], sem.at[slot])

    for i in range(min(n_slots - 1, len(chunks))):
        chunk_copy(i).start()
    for i, (_, dst, r0, c0, cols) in enumerate(chunks):
        nxt = i + n_slots - 1
        if nxt < len(chunks):
            chunk_copy(nxt).start()
        chunk_copy(i).wait()
        dst[r0:r0 + rows, c0:c0 + cols] = ring_ref[i % n_slots, :, 0:cols].astype(BF16)


def _prompt_body(x_ref, xd_ref, xs_ref, mixs_ref, n1_ref, bias_ref, sink_ref, pscale_ref, n2_ref, fn_ref,
                 kc_ref, vc_ref, knewt_ref, vnewt_ref,
                 win_hbm, wpool_hbm, wout_hbm, wg_hbm, wu_hbm, wd_hbm,
                 y_ref, kout_ref, vout_ref, pout_ref, ys_ref, nkc_ref, nvc_ref,
                 win_ref, wpool_ref, wout_ref, wg_ref, wu_ref, wd_ref, pool_stage_ref, dma_sem,
                 kt_ref, kprev_ref, vv_ref, ubuf_ref, mixed_ref, act_ref, s_ref, m_ref, p_ref,
                 *, tiles_per_seq, n_tiles):
    T = x_ref.shape[1]
    NS = xs_ref.shape[0]
    step = pl.program_id(0)
    t = jnp.minimum(step, n_tiles - 1) % tiles_per_seq
    nblk = T // BLOCK
    ROWS = 2 * BLOCK
    KEYS = 2 * BLOCK

    @pl.when(t == 0)
    def _():
        kprev_ref[...] = jnp.zeros(kprev_ref.shape, BF16)
        vv_ref[:, 0:BLOCK, :] = jnp.zeros((4, BLOCK, LANES), BF16)
        ubuf_ref[0:HIST_PAD, :] = jnp.zeros((HIST_PAD, ubuf_ref.shape[1]), F32)

    def update_sample_caches():
        CS, _, L = kc_ref.shape
        last_lane = lax.broadcasted_iota(jnp.int32, (KV_WIDTH, L), 1) == L - 1
        first_sample = step * CS
        for new_t_ref, cache_ref, out_ref in ((knewt_ref, kc_ref, nkc_ref), (vnewt_ref, vc_ref, nvc_ref)):
            new_t = new_t_ref[...]
            for i in range(CS):
                new_col = pltpu.roll(new_t, L - 1 - (first_sample + i), axis=1)
                out_ref[i] = jnp.where(last_lane, new_col, pltpu.roll(cache_ref[i], L - 1, axis=1))

    def mixer():
        return _mixer_phases(x_ref, n1_ref, win_ref, bias_ref, sink_ref, wpool_ref, pscale_ref,
                             kout_ref, vout_ref, pout_ref, kt_ref, kprev_ref, vv_ref, ubuf_ref, mixed_ref,
                             s_ref, m_ref, p_ref, t, jnp.minimum(step, n_tiles - 1) // tiles_per_seq)

    chunks = [slice(c0, c1) for c0, c1 in _ffn_chunks(wg_ref.shape[1])]

    @pl.when(step == 0)
    def _():
        pool_copy = pltpu.make_async_copy(wpool_hbm.at[0], pool_stage_ref, dma_sem.at[s_ref.shape[0]])
        pool_copy.start()
        _load_weights_as_bf16([(win_hbm, win_ref), (wout_hbm, wout_ref), (wg_hbm, wg_ref), (wu_hbm, wu_ref),
                               (wd_hbm, wd_ref)], s_ref, dma_sem)
        pool_copy.wait()
        zp = jnp.zeros((LANES, LANES), BF16)
        for i in range(wpool_ref.shape[0]):
            wa = pool_stage_ref[2 * i].astype(BF16)
            wb = pool_stage_ref[2 * i + 1].astype(BF16)
            wpool_ref[i] = jnp.concatenate([jnp.concatenate([wa, zp], axis=1),
                                            jnp.concatenate([zp, wb], axis=1)], axis=0)
        update_sample_caches()
        sample_act = act_ref.at[0:NS]
        x1, h2 = _dense_open(xs_ref[:, 0, :], mixs_ref[...], wout_ref, n2_ref)
        for cols in chunks:
            _ffn_chunk(h2, cols, wg_ref, wu_ref, sample_act)
        ys_ref[:, 0, :] = _dense_close(x1, sample_act[...], wd_ref, fn_ref)
        for _ in mixer():
            pass

    @pl.when(step == n_tiles)
    def _():
        x1, h2 = _dense_open(xd_ref[0], mixed_ref[...], wout_ref, n2_ref)
        for cols in chunks:
            _ffn_chunk(h2, cols, wg_ref, wu_ref, act_ref)
        y_ref[0] = _dense_close(x1, act_ref[...], wd_ref, fn_ref)

    @pl.when(jnp.logical_and(step > 0, step < n_tiles))
    def _():
        update_sample_caches()
        x1, h2 = _dense_open(xd_ref[0], mixed_ref[...], wout_ref, n2_ref)
        assert len(chunks) == 6
        phases = mixer()
        proj_done = next(phases)
        _ffn_chunk(h2, chunks[0], wg_ref, wu_ref, act_ref)
        _ffn_chunk(h2, chunks[1], wg_ref, wu_ref, act_ref)
        _ffn_chunk(_after(h2, proj_done), chunks[2], wg_ref, wu_ref, act_ref)
        score_done = next(phases)
        _ffn_chunk(h2, chunks[3], wg_ref, wu_ref, act_ref)
        _ffn_chunk(_after(h2, score_done), chunks[4], wg_ref, wu_ref, act_ref)
        _ffn_chunk(h2, chunks[5], wg_ref, wu_ref, act_ref)
        exp_done = next(phases)
        act = _after(act_ref[...], exp_done)
        next(phases)
        y_ref[0] = _dense_close(x1, act, wd_ref, fn_ref)


def _mixer_phases(x_ref, n1_ref, win_ref, bias_ref, sink_ref, wpool_ref, pscale_ref,
                  kout_ref, vout_ref, pout_ref, kt_ref, kprev_ref, vv_ref, ubuf_ref, mixed_ref,
                  s_ref, m_ref, p_ref, t, b):
    T = x_ref.shape[1]
    nblk = T // BLOCK
    ROWS = 2 * BLOCK
    KEYS = 2 * BLOCK

    x = x_ref[0]
    h = _rms(x, n1_ref[...]).astype(BF16)
    proj = _dot(h, win_ref[...])
    yield [proj[T - SUBLANES:, proj.shape[1] - LANES:]]
    q = (proj[:, 0:ATTN_WIDTH] * ATTN_SCALE).astype(BF16)
    k = proj[:, ATTN_WIDTH:ATTN_WIDTH + KV_WIDTH]
    v = proj[:, ATTN_WIDTH + KV_WIDTH:ATTN_WIDTH + 2 * KV_WIDTH]
    u = proj[:, ATTN_WIDTH + 2 * KV_WIDTH:]

    ubuf_ref[HIST_PAD:, :] = u
    pout_ref[:, pl.ds(b, 1), :] = ubuf_ref[HIST_PAD + T - POOL_HIST:HIST_PAD + T, :][:, None, :]

    k_t = k.T
    kt_ref[...] = k_t.astype(BF16)
    kout_ref[0] = k_t[:, T - WINDOW:]
    vout_ref[0] = v[T - WINDOW:, :].T
    for idx, part in enumerate(_split_heads_to_lane_halves(v)):
        vv_ref[idx, BLOCK:, :] = part

    first_tile = t == 0
    before_start = lax.broadcasted_iota(jnp.int32, (ROWS, 2 * KEYS), 1) % KEYS < BLOCK
    score_done = []
    no_keys = jnp.zeros((HEAD_DIM, KEYS), BF16)
    for j in range(nblk):
        rows = slice(j * BLOCK, (j + 1) * BLOCK)
        for g in range(N_KV_HEADS):
            q2 = jnp.concatenate([q[rows, (2 * g) * LANES:(2 * g + 1) * LANES],
                                  q[rows, (2 * g + 1) * LANES:(2 * g + 2) * LANES]], axis=0)
            dims = slice(g * HEAD_DIM, (g + 1) * HEAD_DIM)
            if j == 0:
                kg = jnp.concatenate([kprev_ref[dims, :], kt_ref[dims, 0:BLOCK]], axis=1)
            else:
                kg = kt_ref[dims, (j - 1) * BLOCK:(j + 1) * BLOCK]
            kmat = jnp.concatenate([jnp.concatenate([kg, no_keys], axis=1),
                                    jnp.concatenate([no_keys, kg], axis=1)], axis=0)
            bias = bias_ref[g]
            if j == 0:
                bias = jnp.where(jnp.logical_and(first_tile, before_start), NEG_INF, bias)
            sc = _dot(q2, kmat) + bias
            s_ref[j * N_KV_HEADS + g] = sc
            score_done.append(sc[ROWS - SUBLANES:, 2 * KEYS - LANES:])
    yield score_done

    top = lax.broadcasted_iota(jnp.int32, (ROWS, LANES), 0) < BLOCK
    sink_b = [[jnp.where(top, sink_ref[GQA_GROUP * g + c], sink_ref[GQA_GROUP * g + 2 + c]) for c in range(2)]
              for g in range(N_KV_HEADS)]
    n_units = nblk * N_KV_HEADS
    for un in range(n_units):
        for c in range(2):
            sh = s_ref[un, :, c * KEYS:(c + 1) * KEYS]
            m_ref[un, c] = jnp.maximum(jnp.max(sh, axis=-1, keepdims=True), sink_b[un % N_KV_HEADS][c])
    exp_done = []
    for un in range(n_units):
        for c in range(2):
            sh = s_ref[un, :, c * KEYS:(c + 1) * KEYS]
            m = m_ref[un, c]
            e = jnp.exp(sh - jnp.concatenate([m, m], axis=1))
            p_ref[un, :, c * KEYS:(c + 1) * KEYS] = e.astype(BF16)
            exp_done.append(e[ROWS - SUBLANES:, KEYS - LANES:])
    yield exp_done

    lo = lax.broadcasted_iota(jnp.int32, (ROWS, LANES), 1) < HEAD_DIM
    first_head_rows = lax.broadcasted_iota(jnp.int32, (2 * KEYS, LANES), 0) < KEYS
    first_head_cols = lax.broadcasted_iota(jnp.int32, (2 * KEYS, LANES), 1) < HEAD_DIM
    ones_cols = jnp.where(first_head_rows == first_head_cols, 1.0, 0.0).astype(BF16)
    for un in range(n_units):
        j, g = divmod(un, N_KV_HEADS)
        keys = slice(j * BLOCK, j * BLOCK + KEYS)
        vmat = jnp.concatenate([vv_ref[2 * g, keys, :], vv_ref[2 * g + 1, keys, :]], axis=0)
        o = _dot(p_ref[un], jnp.concatenate([vmat, ones_cols], axis=1))
        m_sel = jnp.where(lo, m_ref[un, 0], m_ref[un, 1])
        sink_sel = jnp.where(lo, sink_b[g][0], sink_b[g][1])
        out = o[:, 0:LANES] / (o[:, LANES:2 * LANES] + jnp.exp(sink_sel - m_sel))
        rows = slice(j * BLOCK, (j + 1) * BLOCK)
        mixed_ref[rows, (2 * g) * LANES:(2 * g + 1) * LANES] = out[0:BLOCK].astype(BF16)
        mixed_ref[rows, (2 * g + 1) * LANES:(2 * g + 2) * LANES] = out[BLOCK:ROWS].astype(BF16)

    pos = lax.broadcasted_iota(jnp.int32, (T, 1), 0) + t * T
    pooled = []
    for g, w in enumerate(POOL_WINDOWS):
        cols = slice(g * LANES, (g + 1) * LANES)
        cur = ubuf_ref[HIST_PAD:HIST_PAD + T, cols]
        win = cur
        for back in range(1, w):
            win = win + ubuf_ref[HIST_PAD - back:HIST_PAD - back + T, cols]
        count = jnp.minimum(w, pos + 1).astype(F32)
        pooled.append((win / count - cur).astype(BF16))
    for g2 in range(len(POOL_WINDOWS) // 2):
        cols = slice(2 * g2 * LANES, (2 * g2 + 2) * LANES)
        m2 = jnp.concatenate(pooled[2 * g2:2 * g2 + 2], axis=1)
        yg = _dot(m2, wpool_ref[g2]) * pscale_ref[:, cols]
        mixed_ref[:, ATTN_WIDTH + 2 * g2 * LANES:ATTN_WIDTH + (2 * g2 + 2) * LANES] = yg.astype(BF16)

    kprev_ref[...] = kt_ref[:, T - BLOCK:T]
    vv_ref[:, 0:BLOCK, :] = vv_ref[:, T:T + BLOCK, :]
    ubuf_ref[0:HIST_PAD, :] = ubuf_ref[T:T + HIST_PAD, :]
    yield None


def _sample_body(xs_ref, n1_ref, win_ref, kt_ref, vt_ref, hist_ref, bias_ref, sink_ref, wpool_ref, pscale_ref,
                 mix_ref, knewt_ref, vnewt_ref, npool_ref,
                 proj_ref, mixed_ref):
    SB = kt_ref.shape[0]
    L = kt_ref.shape[2]
    i = pl.program_id(0)

    @pl.when(i == 0)
    def _():
        h = _rms(xs_ref[:, 0, :], n1_ref[...]).astype(BF16)
        proj = _dot(h, win_ref[0].astype(BF16))
        proj_ref[...] = proj
        knewt_ref[...] = proj[:, ATTN_WIDTH:ATTN_WIDTH + KV_WIDTH].T
        vnewt_ref[...] = proj[:, ATTN_WIDTH + KV_WIDTH:ATTN_WIDTH + 2 * KV_WIDTH].T

    r0 = pl.multiple_of(i * SB, SB)
    pr = proj_ref[pl.ds(r0, SB), :]
    q = pr[:, 0:ATTN_WIDTH] * ATTN_SCALE
    knew = pr[:, ATTN_WIDTH:ATTN_WIDTH + KV_WIDTH]
    vnew = pr[:, ATTN_WIDTH + KV_WIDTH:ATTN_WIDTH + 2 * KV_WIDTH]
    u = pr[:, ATTN_WIDTH + 2 * KV_WIDTH:]

    npool_ref[0:POOL_HIST - 1] = hist_ref[1:POOL_HIST]
    npool_ref[POOL_HIST - 1] = u

    lo_blk = lax.broadcasted_iota(jnp.int32, (SB, LANES), 1) < HEAD_DIM
    zero_blk = jnp.zeros((SB, LANES), F32)
    qh = []
    for p in range(N_PAIRS):
        pair = q[:, p * LANES:(p + 1) * LANES]
        swapped = pltpu.roll(pair, HEAD_DIM, axis=1)
        if (2 * p) // GQA_GROUP == 0:
            qh += [jnp.where(lo_blk, pair, zero_blk), jnp.where(lo_blk, swapped, zero_blk)]
        else:
            qh += [jnp.where(lo_blk, zero_blk, swapped), jnp.where(lo_blk, zero_blk, pair)]

    rows64 = N_HEADS * SUBLANES
    sub = lax.broadcasted_iota(jnp.int32, (rows64, LANES), 0) % SUBLANES
    lo = lax.broadcasted_iota(jnp.int32, (SUBLANES, LANES), 1) < HEAD_DIM
    n_groups = SB // SUBLANES
    groups = [slice(r * SUBLANES, (r + 1) * SUBLANES) for r in range(n_groups)]
    lhs = jnp.concatenate([qh[h][grp] for grp in groups for h in range(N_HEADS)], axis=0)
    lhs_bf = lhs.astype(BF16)
    scores = []
    for r in range(n_groups):
        s = jnp.zeros((rows64, L), F32)
        for si in range(SUBLANES):
            kb = kt_ref[r * SUBLANES + si].astype(BF16)
            s = jnp.where(sub == si, _dot(lhs_bf[r * rows64:(r + 1) * rows64], kb), s)
        scores.append(s)
    s = jnp.concatenate(scores, axis=0) + jnp.concatenate([bias_ref[...]] * n_groups, axis=0)
    sink = jnp.concatenate([jnp.full((SUBLANES, 1), sink_ref[h], F32) for h in range(N_HEADS)] * n_groups, axis=0)
    knew_rep = jnp.concatenate([knew[grp] for grp in groups for _ in range(N_HEADS)], axis=0)
    vnew_rep = jnp.concatenate([vnew[grp] for grp in groups for _ in range(N_HEADS)], axis=0)
    s_new = jnp.sum(lhs * knew_rep, axis=-1, keepdims=True)
    m = jnp.maximum(jnp.maximum(jnp.max(s, axis=-1, keepdims=True), s_new), sink)
    e = jnp.exp(s - m)
    e_new = jnp.exp(s_new - m)
    denom = jnp.sum(e, axis=-1, keepdims=True) + e_new + jnp.exp(sink - m)
    e_bf = e.astype(BF16)
    outs = []
    for r in range(n_groups):
        o = jnp.zeros((rows64, LANES), F32)
        for si in range(SUBLANES):
            vb = vt_ref[r * SUBLANES + si].astype(BF16)
            o = jnp.where(sub == si, _dot_nt(e_bf[r * rows64:(r + 1) * rows64], vb), o)
        outs.append(o)
    o_all = (jnp.concatenate(outs, axis=0) + e_new * vnew_rep) / denom
    for r in range(n_groups):
        o = o_all[r * rows64:(r + 1) * rows64]
        out_rows = pl.ds(pl.multiple_of(r0 + r * SUBLANES, SUBLANES), SUBLANES)
        for p in range(N_PAIRS):
            a = o[2 * p * SUBLANES:(2 * p + 1) * SUBLANES]
            b = o[(2 * p + 1) * SUBLANES:(2 * p + 2) * SUBLANES]
            if (2 * p) // GQA_GROUP == 0:
                pair = jnp.where(lo, a, pltpu.roll(b, HEAD_DIM, axis=1))
            else:
                pair = jnp.where(lo, pltpu.roll(a, HEAD_DIM, axis=1), b)
            mixed_ref[out_rows, p * LANES:(p + 1) * LANES] = pair

    blk_rows = pl.ds(r0, SB)
    for g, w in enumerate(POOL_WINDOWS):
        cols = slice(g * LANES, (g + 1) * LANES)
        cur = u[:, cols]
        win = cur
        for back in range(1, w):
            win = win + hist_ref[POOL_HIST - back, :, cols]
        count = float(min(w, PAST_LEN + 1))
        m = (win / count - cur).astype(BF16)
        yg = _dot(m, wpool_ref[0, g].astype(BF16)) * pscale_ref[:, cols]
        mixed_ref[blk_rows, ATTN_WIDTH + g * LANES:ATTN_WIDTH + (g + 1) * LANES] = yg

    @pl.when(i == pl.num_programs(0) - 1)
    def _():
        mix_ref[...] = mixed_ref[...].astype(BF16)


def _alibi_slopes():
    return np.exp2(-8.0 * np.arange(1, N_HEADS + 1, dtype=np.float32) / N_HEADS).astype(np.float32)


def _prompt_bias():
    slopes = _alibi_slopes()
    qi = np.arange(BLOCK)[:, None]
    ki = np.arange(2 * BLOCK)[None, :]
    rel = BLOCK + qi - ki
    band = (rel >= 0) & (rel <= WINDOW)
    out = np.empty((N_KV_HEADS, 2 * BLOCK, 4 * BLOCK), np.float32)
    for g in range(N_KV_HEADS):
        for r in range(2):
            for c in range(2):
                slope = slopes[GQA_GROUP * g + 2 * r + c]
                val = np.where(band, -slope * rel.astype(np.float32), np.float32(NEG_INF))
                out[g, r * BLOCK:(r + 1) * BLOCK, c * 2 * BLOCK:(c + 1) * 2 * BLOCK] = val
    return out


def _sample_bias(L):
    assert PAST_LEN >= L and L <= WINDOW
    slopes = _alibi_slopes()
    rel = (L - np.arange(L)).astype(np.float32)
    return np.repeat(-slopes[:, None] * rel[None, :], SUBLANES, axis=0).astype(np.float32)


def _resident(shape):
    zeros = (0,) * len(shape)
    return pl.BlockSpec(shape, lambda *_: zeros, pipeline_mode=pl.Buffered(1))


def kernel(x_prompt, x_sample, cache_k_window, cache_v_window, state_pool, norm1, w_in, attn_sinks, w_pool,
           pool_scale, w_out, norm2, w_gate, w_up, w_down, final_norm):
    B, S, D = x_prompt.shape
    NS = x_sample.shape[0]
    L = cache_k_window.shape[2]
    d_ff = w_gate.shape[-1]
    pool_w = state_pool.shape[-1]
    assert norm1.shape[0] == 1 and x_sample.shape[1] == 1
    assert w_in.shape[-1] == ATTN_WIDTH + 2 * KV_WIDTH + pool_w and pool_w == len(POOL_WINDOWS) * LANES
    assert S % PROMPT_TILE == 0 and NS % SAMPLE_BLOCK == 0 and state_pool.shape[2] == POOL_HIST
    T = PROMPT_TILE

    assert D % (2 * BLOCK) == 0 and d_ff % (2 * BLOCK) == 0 and NS <= T
    in_w = w_in.shape[-1]
    n1 = norm1[0].reshape(1, D)
    n2 = norm2[0].reshape(1, D)
    fn = final_norm.reshape(1, D)
    pscale = pool_scale[0].reshape(1, pool_w)
    sinks = attn_sinks[0]
    xs = x_sample
    mixed_w = ATTN_WIDTH + pool_w

    def cache_to_kernel(c):
        return jnp.transpose(c[0], (0, 2, 3, 1)).reshape(NS, KV_WIDTH, L)

    def cache_from_kernel(c):
        n, _, keys = c.shape
        return jnp.transpose(c.reshape(n, N_KV_HEADS, HEAD_DIM, keys), (0, 3, 1, 2))[None]

    SB = SAMPLE_BLOCK
    kt_cache = cache_to_kernel(cache_k_window)
    vt_cache = cache_to_kernel(cache_v_window)
    mix_sample, knew_t, vnew_t, npool = pl.pallas_call(
        _sample_body,
        grid=(NS // SB,),
        in_specs=[_resident((NS, 1, D)),
                  _resident((1, D)),
                  _resident(w_in.shape),
                  pl.BlockSpec((SB, KV_WIDTH, L), lambda i: (i, 0, 0)),
                  pl.BlockSpec((SB, KV_WIDTH, L), lambda i: (i, 0, 0)),
                  pl.BlockSpec((POOL_HIST, SB, pool_w), lambda i: (0, i, 0)),
                  _resident((N_HEADS * SUBLANES, L)),
                  pl.BlockSpec(memory_space=pltpu.SMEM),
                  _resident(w_pool.shape),
                  _resident((1, pool_w))],
        out_specs=[pl.BlockSpec((NS, mixed_w), lambda i: (0, 0)),
                   pl.BlockSpec((KV_WIDTH, NS), lambda i: (0, 0)),
                   pl.BlockSpec((KV_WIDTH, NS), lambda i: (0, 0)),
                   pl.BlockSpec((POOL_HIST, SB, pool_w), lambda i: (0, i, 0))],
        out_shape=[jax.ShapeDtypeStruct((NS, mixed_w), BF16),
                   jax.ShapeDtypeStruct((KV_WIDTH, NS), F32),
                   jax.ShapeDtypeStruct((KV_WIDTH, NS), F32),
                   jax.ShapeDtypeStruct((POOL_HIST, NS, pool_w), F32)],
        scratch_shapes=[pltpu.VMEM((NS, in_w), F32),
                        pltpu.VMEM((NS, mixed_w), F32)],
        compiler_params=pltpu.CompilerParams(dimension_semantics=("arbitrary",),
                                             vmem_limit_bytes=VMEM_LIMIT_BYTES),
        name="sample_layer",
    )(xs, n1, w_in, kt_cache, vt_cache,
      jnp.transpose(state_pool[0], (1, 0, 2)), jnp.asarray(_sample_bias(L)), sinks, w_pool, pscale)

    n_units = (T // BLOCK) * N_KV_HEADS
    tiles_per_seq = S // T
    n_tiles = B * tiles_per_seq

    def mixer_tile(s):
        a = jnp.minimum(s, n_tiles - 1)
        return a // tiles_per_seq, a % tiles_per_seq

    def dense_tile(s):
        a = jnp.maximum(s - 1, 0)
        return a // tiles_per_seq, a % tiles_per_seq

    hbm = pl.BlockSpec(memory_space=pl.ANY)
    assert NS % n_tiles == 0 and NS == L == LANES
    cache_blk = pl.BlockSpec((NS // n_tiles, KV_WIDTH, L), lambda s: (jnp.minimum(s, n_tiles - 1), 0, 0))
    y_prompt, k_last, v_last, pool_last, y_sample, nkt, nvt = pl.pallas_call(
        functools.partial(_prompt_body, tiles_per_seq=tiles_per_seq, n_tiles=n_tiles),
        grid=(n_tiles + 1,),
        in_specs=[pl.BlockSpec((1, T, D), lambda s: (*mixer_tile(s), 0)),
                  pl.BlockSpec((1, T, D), lambda s: (*dense_tile(s), 0)),
                  _resident((NS, 1, D)),
                  _resident((NS, mixed_w)),
                  _resident((1, D)),
                  _resident((N_KV_HEADS, 2 * BLOCK, 4 * BLOCK)),
                  pl.BlockSpec(memory_space=pltpu.SMEM),
                  _resident((1, pool_w)),
                  _resident((1, D)),
                  _resident((1, D)),
                  cache_blk, cache_blk,
                  _resident((KV_WIDTH, NS)),
                  _resident((KV_WIDTH, NS)),
                  hbm, hbm, hbm, hbm, hbm, hbm],
        out_specs=[pl.BlockSpec((1, T, D), lambda s: (*dense_tile(s), 0)),
                   pl.BlockSpec((1, KV_WIDTH, WINDOW), lambda s: (mixer_tile(s)[0], 0, 0)),
                   pl.BlockSpec((1, KV_WIDTH, WINDOW), lambda s: (mixer_tile(s)[0], 0, 0)),
                   pl.BlockSpec((POOL_HIST, B, pool_w), lambda s: (0, 0, 0)),
                   pl.BlockSpec((NS, 1, D), lambda s: (0, 0, 0)),
                   cache_blk, cache_blk],
        out_shape=[jax.ShapeDtypeStruct((B, S, D), F32),
                   jax.ShapeDtypeStruct((B, KV_WIDTH, WINDOW), F32),
                   jax.ShapeDtypeStruct((B, KV_WIDTH, WINDOW), F32),
                   jax.ShapeDtypeStruct((POOL_HIST, B, pool_w), F32),
                   jax.ShapeDtypeStruct((NS, 1, D), F32),
                   jax.ShapeDtypeStruct((NS, KV_WIDTH, L), F32),
                   jax.ShapeDtypeStruct((NS, KV_WIDTH, L), F32)],
        scratch_shapes=[pltpu.VMEM((D, in_w), BF16),
                        pltpu.VMEM((len(POOL_WINDOWS) // 2, 2 * LANES, 2 * LANES), BF16),
                        pltpu.VMEM((mixed_w, D), BF16),
                        pltpu.VMEM((D, d_ff), BF16),
                        pltpu.VMEM((D, d_ff), BF16),
                        pltpu.VMEM((d_ff, D), BF16),
                        pltpu.VMEM(w_pool.shape[1:], F32),
                        pltpu.SemaphoreType.DMA((n_units + 1,)),
                        pltpu.VMEM((KV_WIDTH, T), BF16),
                        pltpu.VMEM((KV_WIDTH, BLOCK), BF16),
                        pltpu.VMEM((4, T + BLOCK, LANES), BF16),
                        pltpu.VMEM((T + HIST_PAD, pool_w), F32),
                        pltpu.VMEM((T, mixed_w), BF16),
                        pltpu.VMEM((T, d_ff), BF16),
                        pltpu.VMEM((n_units, 2 * BLOCK, 4 * BLOCK), F32),
                        pltpu.VMEM((n_units, 2, 2 * BLOCK, LANES), F32),
                        pltpu.VMEM((n_units, 2 * BLOCK, 4 * BLOCK), BF16)],
        compiler_params=pltpu.CompilerParams(dimension_semantics=("arbitrary",),
                                             vmem_limit_bytes=VMEM_LIMIT_BYTES),
        name="prompt_layer",
    )(x_prompt, x_prompt, xs, mix_sample, n1, jnp.asarray(_prompt_bias()), sinks, pscale, n2, fn,
      kt_cache, vt_cache, knew_t, vnew_t, w_in, w_pool, w_out, w_gate, w_up, w_down)

    return (y_prompt, y_sample, cache_from_kernel(k_last), cache_from_kernel(v_last),
            jnp.transpose(pool_last, (1, 0, 2))[None], cache_from_kernel(nkt), cache_from_kernel(nvt),
            jnp.transpose(npool, (1, 0, 2))[None])
```

```python
import functools

import numpy as np
import jax
import jax.numpy as jnp
from jax import lax
from jax.experimental import pallas as pl
from jax.experimental.pallas import tpu as pltpu

N_HEADS = 8
N_KV_HEADS = 2
HEAD_DIM = 64
GQA_GROUP = N_HEADS // N_KV_HEADS
WINDOW = 128
BLOCK = WINDOW
PAST_LEN = 8192
POOL_WINDOWS = (2, 4, 8, 16)
POOL_HIST = max(POOL_WINDOWS) - 1
RMS_EPS = 1e-5
NEG_INF = -1e30
ATTN_SCALE = HEAD_DIM ** -0.5

ATTN_WIDTH = N_HEADS * HEAD_DIM
KV_WIDTH = N_KV_HEADS * HEAD_DIM
N_PAIRS = N_HEADS // 2
LANES = 128
SUBLANES = 8
MXU_DEPTH = 256
HIST_PAD = 16

PROMPT_TILE = 512
SAMPLE_BLOCK = 64
FFN_CHUNK = 512
VMEM_LIMIT_BYTES = 62 * 1024 * 1024

F32 = jnp.float32
BF16 = jnp.bfloat16


def _rms(x, g):
    var = jnp.mean(x * x, axis=-1, keepdims=True)
    return x * lax.rsqrt(var + RMS_EPS) * g


def _dot(a, b):
    return jnp.dot(a, b, preferred_element_type=F32)


def _dot_nt(a, b):
    return lax.dot_general(a, b, (((1,), (1,)), ((), ())), preferred_element_type=F32)


def _ffn_chunks(d_ff):
    return [(c, min(c + FFN_CHUNK, d_ff)) for c in range(0, d_ff, FFN_CHUNK)]


def _dense_open(x, mixed_bf16, wout_ref, n2_ref):
    x1 = x + _dot(mixed_bf16, wout_ref[...])
    return x1, _rms(x1, n2_ref[...]).astype(BF16)


def _ffn_chunk(h2, cols, wg_ref, wu_ref, act_ref):
    gate = _dot(h2, wg_ref[:, cols])
    up = _dot(h2, wu_ref[:, cols])
    act_ref[:, cols] = (jax.nn.silu(gate) * up).astype(BF16)


def _dense_close(x1, act_bf16, wd_ref, fn_ref):
    return _rms(x1 + _dot(act_bf16, wd_ref[...]), fn_ref[...])


def _after(value, anchors):
    acc = anchors[0]
    for a in anchors[1:]:
        acc = acc + a
    bits = lax.bitcast_convert_type(acc, jnp.int32)
    zero = lax.shift_right_logical(lax.shift_right_logical(bits, 16), 16)[0:1, 0:1]
    head = value[:, 0:MXU_DEPTH] + zero.astype(value.dtype)
    return jnp.concatenate([head, value[:, MXU_DEPTH:]], axis=1)


def _split_heads_to_lane_halves(a):
    lo = lax.broadcasted_iota(jnp.int32, a.shape, 1) < HEAD_DIM
    zero = jnp.zeros_like(a)
    ar = pltpu.roll(a, HEAD_DIM, axis=1)
    return (jnp.where(lo, a, zero).astype(BF16), jnp.where(lo, zero, ar).astype(BF16),
            jnp.where(lo, ar, zero).astype(BF16), jnp.where(lo, zero, a).astype(BF16))


def _load_weights_as_bf16(weights, ring_ref, sem):
    n_slots, rows, max_cols = ring_ref.shape
    chunks = [(src, dst, r0, c0, min(max_cols, dst.shape[1] - c0))
              for src, dst in weights
              for r0 in range(0, dst.shape[0], rows)
              for c0 in range(0, dst.shape[1], max_cols)]

    def chunk_copy(i):
        src, _, r0, c0, cols = chunks[i]
        slot = i % n_slots
        return pltpu.make_async_copy(src.at[0, pl.ds(r0, rows), pl.ds(c0, cols)],
                                     ring_ref.at[slot, :, 0:cols], sem.at[slot])

    for i in range(min(n_slots - 1, len(chunks))):
        chunk_copy(i).start()
    for i, (_, dst, r0, c0, cols) in enumerate(chunks):
        nxt = i + n_slots - 1
        if nxt < len(chunks):
            chunk_copy(nxt).start()
        chunk_copy(i).wait()
        dst[r0:r0 + rows, c0:c0 + cols] = ring_ref[i % n_slots, :, 0:cols].astype(BF16)


def _prompt_body(x_ref, xd_ref, xs_ref, mixs_ref, n1_ref, bias_ref, sink_ref, pscale_ref, n2_ref, fn_ref,
                 kc_ref, vc_ref, knewt_ref, vnewt_ref,
                 win_hbm, wpool_hbm, wout_hbm, wg_hbm, wu_hbm, wd_hbm,
                 y_ref, kout_ref, vout_ref, pout_ref, ys_ref, nkc_ref, nvc_ref,
                 win_ref, wpool_ref, wout_ref, wg_ref, wu_ref, wd_ref, pool_stage_ref, dma_sem,
                 kt_ref, kprev_ref, vv_ref, ubuf_ref, mixed_ref, act_ref, s_ref, m_ref, p_ref,
                 *, tiles_per_seq, n_tiles):
    T = x_ref.shape[1]
    NS = xs_ref.shape[0]
    step = pl.program_id(0)
    t = jnp.minimum(step, n_tiles - 1) % tiles_per_seq
    nblk = T // BLOCK
    ROWS = 2 * BLOCK
    KEYS = 2 * BLOCK

    @pl.when(t == 0)
    def _():
        kprev_ref[...] = jnp.zeros(kprev_ref.shape, BF16)
        vv_ref[:, 0:BLOCK, :] = jnp.zeros((4, BLOCK, LANES), BF16)
        ubuf_ref[0:HIST_PAD, :] = jnp.zeros((HIST_PAD, ubuf_ref.shape[1]), F32)

    def update_sample_caches():
        CS, _, L = kc_ref.shape
        last_lane = lax.broadcasted_iota(jnp.int32, (KV_WIDTH, L), 1) == L - 1
        first_sample = step * CS
        for new_t_ref, cache_ref, out_ref in ((knewt_ref, kc_ref, nkc_ref), (vnewt_ref, vc_ref, nvc_ref)):
            new_t = new_t_ref[...]
            for i in range(CS):
                new_col = pltpu.roll(new_t, L - 1 - (first_sample + i), axis=1)
                out_ref[i] = jnp.where(last_lane, new_col, pltpu.roll(cache_ref[i], L - 1, axis=1))

    def mixer():
        return _mixer_phases(x_ref, n1_ref, win_ref, bias_ref, sink_ref, wpool_ref, pscale_ref,
                             kout_ref, vout_ref, pout_ref, kt_ref, kprev_ref, vv_ref, ubuf_ref, mixed_ref,
                             s_ref, m_ref, p_ref, t, jnp.minimum(step, n_tiles - 1) // tiles_per_seq)

    chunks = [slice(c0, c1) for c0, c1 in _ffn_chunks(wg_ref.shape[1])]

    @pl.when(step == 0)
    def _():
        pool_copy = pltpu.make_async_copy(wpool_hbm.at[0], pool_stage_ref, dma_sem.at[s_ref.shape[0]])
        pool_copy.start()
        _load_weights_as_bf16([(win_hbm, win_ref), (wout_hbm, wout_ref), (wg_hbm, wg_ref), (wu_hbm, wu_ref),
                               (wd_hbm, wd_ref)], s_ref, dma_sem)
        pool_copy.wait()
        zp = jnp.zeros((LANES, LANES), BF16)
        for i in range(wpool_ref.shape[0]):
            wa = pool_stage_ref[2 * i].astype(BF16)
            wb = pool_stage_ref[2 * i + 1].astype(BF16)
            wpool_ref[i] = jnp.concatenate([jnp.concatenate([wa, zp], axis=1),
                                            jnp.concatenate([zp, wb], axis=1)], axis=0)
        update_sample_caches()
        sample_act = act_ref.at[0:NS]
        x1, h2 = _dense_open(xs_ref[:, 0, :], mixs_ref[...], wout_ref, n2_ref)
        for cols in chunks:
            _ffn_chunk(h2, cols, wg_ref, wu_ref, sample_act)
        ys_ref[:, 0, :] = _dense_close(x1, sample_act[...], wd_ref, fn_ref)
        for _ in mixer():
            pass

    @pl.when(step == n_tiles)
    def _():
        x1, h2 = _dense_open(xd_ref[0], mixed_ref[...], wout_ref, n2_ref)
        for cols in chunks:
            _ffn_chunk(h2, cols, wg_ref, wu_ref, act_ref)
        y_ref[0] = _dense_close(x1, act_ref[...], wd_ref, fn_ref)

    @pl.when(jnp.logical_and(step > 0, step < n_tiles))
    def _():
        update_sample_caches()
        x1, h2 = _dense_open(xd_ref[0], mixed_ref[...], wout_ref, n2_ref)
        assert len(chunks) == 6
        phases = mixer()
        proj_done = next(phases)
        _ffn_chunk(h2, chunks[0], wg_ref, wu_ref, act_ref)
        _ffn_chunk(h2, chunks[1], wg_ref, wu_ref, act_ref)
        _ffn_chunk(_after(h2, proj_done), chunks[2], wg_ref, wu_ref, act_ref)
        score_done = next(phases)
        _ffn_chunk(h2, chunks[3], wg_ref, wu_ref, act_ref)
        _ffn_chunk(_after(h2, score_done), chunks[4], wg_ref, wu_ref, act_ref)
        _ffn_chunk(h2, chunks[5], wg_ref, wu_ref, act_ref)
        exp_done = next(phases)
        act = _after(act_ref[...], exp_done)
        next(phases)
        y_ref[0] = _dense_close(x1, act, wd_ref, fn_ref)


def _mixer_phases(x_ref, n1_ref, win_ref, bias_ref, sink_ref, wpool_ref, pscale_ref,
                  kout_ref, vout_ref, pout_ref, kt_ref, kprev_ref, vv_ref, ubuf_ref, mixed_ref,
                  s_ref, m_ref, p_ref, t, b):
    T = x_ref.shape[1]
    nblk = T // BLOCK
    ROWS = 2 * BLOCK
    KEYS = 2 * BLOCK

    x = x_ref[0]
    h = _rms(x, n1_ref[...]).astype(BF16)
    proj = _dot(h, win_ref[...])
    yield [proj[T - SUBLANES:, proj.shape[1] - LANES:]]
    q = (proj[:, 0:ATTN_WIDTH] * ATTN_SCALE).astype(BF16)
    k = proj[:, ATTN_WIDTH:ATTN_WIDTH + KV_WIDTH]
    v = proj[:, ATTN_WIDTH + KV_WIDTH:ATTN_WIDTH + 2 * KV_WIDTH]
    u = proj[:, ATTN_WIDTH + 2 * KV_WIDTH:]

    ubuf_ref[HIST_PAD:, :] = u
    pout_ref[:, pl.ds(b, 1), :] = ubuf_ref[HIST_PAD + T - POOL_HIST:HIST_PAD + T, :][:, None, :]

    k_t = k.T
    kt_ref[...] = k_t.astype(BF16)
    kout_ref[0] = k_t[:, T - WINDOW:]
    vout_ref[0] = v[T - WINDOW:, :].T
    for idx, part in enumerate(_split_heads_to_lane_halves(v)):
        vv_ref[idx, BLOCK:, :] = part

    first_tile = t == 0
    before_start = lax.broadcasted_iota(jnp.int32, (ROWS, 2 * KEYS), 1) % KEYS < BLOCK
    score_done = []
    no_keys = jnp.zeros((HEAD_DIM, KEYS), BF16)
    for j in range(nblk):
        rows = slice(j * BLOCK, (j + 1) * BLOCK)
        for g in range(N_KV_HEADS):
            q2 = jnp.concatenate([q[rows, (2 * g) * LANES:(2 * g + 1) * LANES],
                                  q[rows, (2 * g + 1) * LANES:(2 * g + 2) * LANES]], axis=0)
            dims = slice(g * HEAD_DIM, (g + 1) * HEAD_DIM)
            if j == 0:
                kg = jnp.concatenate([kprev_ref[dims, :], kt_ref[dims, 0:BLOCK]], axis=1)
            else:
                kg = kt_ref[dims, (j - 1) * BLOCK:(j + 1) * BLOCK]
            kmat = jnp.concatenate([jnp.concatenate([kg, no_keys], axis=1),
                                    jnp.concatenate([no_keys, kg], axis=1)], axis=0)
            bias = bias_ref[g]
            if j == 0:
                bias = jnp.where(jnp.logical_and(first_tile, before_start), NEG_INF, bias)
            sc = _dot(q2, kmat) + bias
            s_ref[j * N_KV_HEADS + g] = sc
            score_done.append(sc[ROWS - SUBLANES:, 2 * KEYS - LANES:])
    yield score_done

    top = lax.broadcasted_iota(jnp.int32, (ROWS, LANES), 0) < BLOCK
    sink_b = [[jnp.where(top, sink_ref[GQA_GROUP * g + c], sink_ref[GQA_GROUP * g + 2 + c]) for c in range(2)]
              for g in range(N_KV_HEADS)]
    n_units = nblk * N_KV_HEADS
    for un in range(n_units):
        for c in range(2):
            sh = s_ref[un, :, c * KEYS:(c + 1) * KEYS]
            m_ref[un, c] = jnp.maximum(jnp.max(sh, axis=-1, keepdims=True), sink_b[un % N_KV_HEADS][c])
    exp_done = []
    for un in range(n_units):
        for c in range(2):
            sh = s_ref[un, :, c * KEYS:(c + 1) * KEYS]
            m = m_ref[un, c]
            e = jnp.exp(sh - jnp.concatenate([m, m], axis=1))
            p_ref[un, :, c * KEYS:(c + 1) * KEYS] = e.astype(BF16)
            exp_done.append(e[ROWS - SUBLANES:, KEYS - LANES:])
    yield exp_done

    lo = lax.broadcasted_iota(jnp.int32, (ROWS, LANES), 1) < HEAD_DIM
    first_head_rows = lax.broadcasted_iota(jnp.int32, (2 * KEYS, LANES), 0) < KEYS
    first_head_cols = lax.broadcasted_iota(jnp.int32, (2 * KEYS, LANES), 1) < HEAD_DIM
    ones_cols = jnp.where(first_head_rows == first_head_cols, 1.0, 0.0).astype(BF16)
    for un in range(n_units):
        j, g = divmod(un, N_KV_HEADS)
        keys = slice(j * BLOCK, j * BLOCK + KEYS)
        vmat = jnp.concatenate([vv_ref[2 * g, keys, :], vv_ref[2 * g + 1, keys, :]], axis=0)
        o = _dot(p_ref[un], jnp.concatenate([vmat, ones_cols], axis=1))
        m_sel = jnp.where(lo, m_ref[un, 0], m_ref[un, 1])
        sink_sel = jnp.where(lo, sink_b[g][0], sink_b[g][1])
        out = o[:, 0:LANES] / (o[:, LANES:2 * LANES] + jnp.exp(sink_sel - m_sel))
        rows = slice(j * BLOCK, (j + 1) * BLOCK)
        mixed_ref[rows, (2 * g) * LANES:(2 * g + 1) * LANES] = out[0:BLOCK].astype(BF16)
        mixed_ref[rows, (2 * g + 1) * LANES:(2 * g + 2) * LANES] = out[BLOCK:ROWS].astype(BF16)

    pos = lax.broadcasted_iota(jnp.int32, (T, 1), 0) + t * T
    pooled = []
    for g, w in enumerate(POOL_WINDOWS):
        cols = slice(g * LANES, (g + 1) * LANES)
        cur = ubuf_ref[HIST_PAD:HIST_PAD + T, cols]
        win = cur
        for back in range(1, w):
            win = win + ubuf_ref[HIST_PAD - back:HIST_PAD - back + T, cols]
        count = jnp.minimum(w, pos + 1).astype(F32)
        pooled.append((win / count - cur).astype(BF16))
    for g2 in range(len(POOL_WINDOWS) // 2):
        cols = slice(2 * g2 * LANES, (2 * g2 + 2) * LANES)
        m2 = jnp.concatenate(pooled[2 * g2:2 * g2 + 2], axis=1)
        yg = _dot(m2, wpool_ref[g2]) * pscale_ref[:, cols]
        mixed_ref[:, ATTN_WIDTH + 2 * g2 * LANES:ATTN_WIDTH + (2 * g2 + 2) * LANES] = yg.astype(BF16)

    kprev_ref[...] = kt_ref[:, T - BLOCK:T]
    vv_ref[:, 0:BLOCK, :] = vv_ref[:, T:T + BLOCK, :]
    ubuf_ref[0:HIST_PAD, :] = ubuf_ref[T:T + HIST_PAD, :]
    yield None


def _sample_body(xs_ref, n1_ref, win_ref, kt_ref, vt_ref, hist_ref, bias_ref, sink_ref, wpool_ref, pscale_ref,
                 mix_ref, knewt_ref, vnewt_ref, npool_ref,
                 proj_ref, mixed_ref):
    SB = kt_ref.shape[0]
    L = kt_ref.shape[2]
    i = pl.program_id(0)

    @pl.when(i == 0)
    def _():
        h = _rms(xs_ref[:, 0, :], n1_ref[...]).astype(BF16)
        proj = _dot(h, win_ref[0].astype(BF16))
        proj_ref[...] = proj
        knewt_ref[...] = proj[:, ATTN_WIDTH:ATTN_WIDTH + KV_WIDTH].T
        vnewt_ref[...] = proj[:, ATTN_WIDTH + KV_WIDTH:ATTN_WIDTH + 2 * KV_WIDTH].T

    r0 = pl.multiple_of(i * SB, SB)
    pr = proj_ref[pl.ds(r0, SB), :]
    q = pr[:, 0:ATTN_WIDTH] * ATTN_SCALE
    knew = pr[:, ATTN_WIDTH:ATTN_WIDTH + KV_WIDTH]
    vnew = pr[:, ATTN_WIDTH + KV_WIDTH:ATTN_WIDTH + 2 * KV_WIDTH]
    u = pr[:, ATTN_WIDTH + 2 * KV_WIDTH:]

    npool_ref[0:POOL_HIST - 1] = hist_ref[1:POOL_HIST]
    npool_ref[POOL_HIST - 1] = u

    lo_blk = lax.broadcasted_iota(jnp.int32, (SB, LANES), 1) < HEAD_DIM
    zero_blk = jnp.zeros((SB, LANES), F32)
    qh = []
    for p in range(N_PAIRS):
        pair = q[:, p * LANES:(p + 1) * LANES]
        swapped = pltpu.roll(pair, HEAD_DIM, axis=1)
        if (2 * p) // GQA_GROUP == 0:
            qh += [jnp.where(lo_blk, pair, zero_blk), jnp.where(lo_blk, swapped, zero_blk)]
        else:
            qh += [jnp.where(lo_blk, zero_blk, swapped), jnp.where(lo_blk, zero_blk, pair)]

    rows64 = N_HEADS * SUBLANES
    sub = lax.broadcasted_iota(jnp.int32, (rows64, LANES), 0) % SUBLANES
    lo = lax.broadcasted_iota(jnp.int32, (SUBLANES, LANES), 1) < HEAD_DIM
    n_groups = SB // SUBLANES
    groups = [slice(r * SUBLANES, (r + 1) * SUBLANES) for r in range(n_groups)]
    lhs = jnp.concatenate([qh[h][grp] for grp in groups for h in range(N_HEADS)], axis=0)
    lhs_bf = lhs.astype(BF16)
    scores = []
    for r in range(n_groups):
        s = jnp.zeros((rows64, L), F32)
        for si in range(SUBLANES):
            kb = kt_ref[r * SUBLANES + si].astype(BF16)
            s = jnp.where(sub == si, _dot(lhs_bf[r * rows64:(r + 1) * rows64], kb), s)
        scores.append(s)
    s = jnp.concatenate(scores, axis=0) + jnp.concatenate([bias_ref[...]] * n_groups, axis=0)
    sink = jnp.concatenate([jnp.full((SUBLANES, 1), sink_ref[h], F32) for h in range(N_HEADS)] * n_groups, axis=0)
    knew_rep = jnp.concatenate([knew[grp] for grp in groups for _ in range(N_HEADS)], axis=0)
    vnew_rep = jnp.concatenate([vnew[grp] for grp in groups for _ in range(N_HEADS)], axis=0)
    s_new = jnp.sum(lhs * knew_rep, axis=-1, keepdims=True)
    m = jnp.maximum(jnp.maximum(jnp.max(s, axis=-1, keepdims=True), s_new), sink)
    e = jnp.exp(s - m)
    e_new = jnp.exp(s_new - m)
    denom = jnp.sum(e, axis=-1, keepdims=True) + e_new + jnp.exp(sink - m)
    e_bf = e.astype(BF16)
    outs = []
    for r in range(n_groups):
        o = jnp.zeros((rows64, LANES), F32)
        for si in range(SUBLANES):
            vb = vt_ref[r * SUBLANES + si].astype(BF16)
            o = jnp.where(sub == si, _dot_nt(e_bf[r * rows64:(r + 1) * rows64], vb), o)
        outs.append(o)
    o_all = (jnp.concatenate(outs, axis=0) + e_new * vnew_rep) / denom
    for r in range(n_groups):
        o = o_all[r * rows64:(r + 1) * rows64]
        out_rows = pl.ds(pl.multiple_of(r0 + r * SUBLANES, SUBLANES), SUBLANES)
        for p in range(N_PAIRS):
            a = o[2 * p * SUBLANES:(2 * p + 1) * SUBLANES]
            b = o[(2 * p + 1) * SUBLANES:(2 * p + 2) * SUBLANES]
            if (2 * p) // GQA_GROUP == 0:
                pair = jnp.where(lo, a, pltpu.roll(b, HEAD_DIM, axis=1))
            else:
                pair = jnp.where(lo, pltpu.roll(a, HEAD_DIM, axis=1), b)
            mixed_ref[out_rows, p * LANES:(p + 1) * LANES] = pair

    blk_rows = pl.ds(r0, SB)
    for g, w in enumerate(POOL_WINDOWS):
        cols = slice(g * LANES, (g + 1) * LANES)
        cur = u[:, cols]
        win = cur
        for back in range(1, w):
            win = win + hist_ref[POOL_HIST - back, :, cols]
        count = float(min(w, PAST_LEN + 1))
        m = (win / count - cur).astype(BF16)
        yg = _dot(m, wpool_ref[0, g].astype(BF16)) * pscale_ref[:, cols]
        mixed_ref[blk_rows, ATTN_WIDTH + g * LANES:ATTN_WIDTH + (g + 1) * LANES] = yg

    @pl.when(i == pl.num_programs(0) - 1)
    def _():
        mix_ref[...] = mixed_ref[...].astype(BF16)


def _alibi_slopes():
    return np.exp2(-8.0 * np.arange(1, N_HEADS + 1, dtype=np.float32) / N_HEADS).astype(np.float32)


def _prompt_bias():
    slopes = _alibi_slopes()
    qi = np.arange(BLOCK)[:, None]
    ki = np.arange(2 * BLOCK)[None, :]
    rel = BLOCK + qi - ki
    band = (rel >= 0) & (rel <= WINDOW)
    out = np.empty((N_KV_HEADS, 2 * BLOCK, 4 * BLOCK), np.float32)
    for g in range(N_KV_HEADS):
        for r in range(2):
            for c in range(2):
                slope = slopes[GQA_GROUP * g + 2 * r + c]
                val = np.where(band, -slope * rel.astype(np.float32), np.float32(NEG_INF))
                out[g, r * BLOCK:(r + 1) * BLOCK, c * 2 * BLOCK:(c + 1) * 2 * BLOCK] = val
    return out


def _sample_bias(L):
    assert PAST_LEN >= L and L <= WINDOW
    slopes = _alibi_slopes()
    rel = (L - np.arange(L)).astype(np.float32)
    return np.repeat(-slopes[:, None] * rel[None, :], SUBLANES, axis=0).astype(np.float32)


def _resident(shape):
    zeros = (0,) * len(shape)
    return pl.BlockSpec(shape, lambda *_: zeros, pipeline_mode=pl.Buffered(1))


def kernel(x_prompt, x_sample, cache_k_window, cache_v_window, state_pool, norm1, w_in, attn_sinks, w_pool,
           pool_scale, w_out, norm2, w_gate, w_up, w_down, final_norm):
    B, S, D = x_prompt.shape
    NS = x_sample.shape[0]
    L = cache_k_window.shape[2]
    d_ff = w_gate.shape[-1]
    pool_w = state_pool.shape[-1]
    assert norm1.shape[0] == 1 and x_sample.shape[1] == 1
    assert w_in.shape[-1] == ATTN_WIDTH + 2 * KV_WIDTH + pool_w and pool_w == len(POOL_WINDOWS) * LANES
    assert S % PROMPT_TILE == 0 and NS % SAMPLE_BLOCK == 0 and state_pool.shape[2] == POOL_HIST
    T = PROMPT_TILE

    assert D % (2 * BLOCK) == 0 and d_ff % (2 * BLOCK) == 0 and NS <= T
    in_w = w_in.shape[-1]
    n1 = norm1[0].reshape(1, D)
    n2 = norm2[0].reshape(1, D)
    fn = final_norm.reshape(1, D)
    pscale = pool_scale[0].reshape(1, pool_w)
    sinks = attn_sinks[0]
    xs = x_sample
    mixed_w = ATTN_WIDTH + pool_w

    def cache_to_kernel(c):
        return jnp.transpose(c[0], (0, 2, 3, 1)).reshape(NS, KV_WIDTH, L)

    def cache_from_kernel(c):
        n, _, keys = c.shape
        return jnp.transpose(c.reshape(n, N_KV_HEADS, HEAD_DIM, keys), (0, 3, 1, 2))[None]

    SB = SAMPLE_BLOCK
    kt_cache = cache_to_kernel(cache_k_window)
    vt_cache = cache_to_kernel(cache_v_window)
    mix_sample, knew_t, vnew_t, npool = pl.pallas_call(
        _sample_body,
        grid=(NS // SB,),
        in_specs=[_resident((NS, 1, D)),
                  _resident((1, D)),
                  _resident(w_in.shape),
                  pl.BlockSpec((SB, KV_WIDTH, L), lambda i: (i, 0, 0)),
                  pl.BlockSpec((SB, KV_WIDTH, L), lambda i: (i, 0, 0)),
                  pl.BlockSpec((POOL_HIST, SB, pool_w), lambda i: (0, i, 0)),
                  _resident((N_HEADS * SUBLANES, L)),
                  pl.BlockSpec(memory_space=pltpu.SMEM),
                  _resident(w_pool.shape),
                  _resident((1, pool_w))],
        out_specs=[pl.BlockSpec((NS, mixed_w), lambda i: (0, 0)),
                   pl.BlockSpec((KV_WIDTH, NS), lambda i: (0, 0)),
                   pl.BlockSpec((KV_WIDTH, NS), lambda i: (0, 0)),
                   pl.BlockSpec((POOL_HIST, SB, pool_w), lambda i: (0, i, 0))],
        out_shape=[jax.ShapeDtypeStruct((NS, mixed_w), BF16),
                   jax.ShapeDtypeStruct((KV_WIDTH, NS), F32),
                   jax.ShapeDtypeStruct((KV_WIDTH, NS), F32),
                   jax.ShapeDtypeStruct((POOL_HIST, NS, pool_w), F32)],
        scratch_shapes=[pltpu.VMEM((NS, in_w), F32),
                        pltpu.VMEM((NS, mixed_w), F32)],
        compiler_params=pltpu.CompilerParams(dimension_semantics=("arbitrary",),
                                             vmem_limit_bytes=VMEM_LIMIT_BYTES),
        name="sample_layer",
    )(xs, n1, w_in, kt_cache, vt_cache,
      jnp.transpose(state_pool[0], (1, 0, 2)), jnp.asarray(_sample_bias(L)), sinks, w_pool, pscale)

    n_units = (T // BLOCK) * N_KV_HEADS
    tiles_per_seq = S // T
    n_tiles = B * tiles_per_seq

    def mixer_tile(s):
        a = jnp.minimum(s, n_tiles - 1)
        return a // tiles_per_seq, a % tiles_per_seq

    def dense_tile(s):
        a = jnp.maximum(s - 1, 0)
        return a // tiles_per_seq, a % tiles_per_seq

    hbm = pl.BlockSpec(memory_space=pl.ANY)
    assert NS % n_tiles == 0 and NS == L == LANES
    cache_blk = pl.BlockSpec((NS // n_tiles, KV_WIDTH, L), lambda s: (jnp.minimum(s, n_tiles - 1), 0, 0))
    y_prompt, k_last, v_last, pool_last, y_sample, nkt, nvt = pl.pallas_call(
        functools.partial(_prompt_body, tiles_per_seq=tiles_per_seq, n_tiles=n_tiles),
        grid=(n_tiles + 1,),
        in_specs=[pl.BlockSpec((1, T, D), lambda s: (*mixer_tile(s), 0)),
                  pl.BlockSpec((1, T, D), lambda s: (*dense_tile(s), 0)),
                  _resident((NS, 1, D)),
                  _resident((NS, mixed_w)),
                  _resident((1, D)),
                  _resident((N_KV_HEADS, 2 * BLOCK, 4 * BLOCK)),
                  pl.BlockSpec(memory_space=pltpu.SMEM),
                  _resident((1, pool_w)),
                  _resident((1, D)),
                  _resident((1, D)),
                  cache_blk, cache_blk,
                  _resident((KV_WIDTH, NS)),
                  _resident((KV_WIDTH, NS)),
                  hbm, hbm, hbm, hbm, hbm, hbm],
        out_specs=[pl.BlockSpec((1, T, D), lambda s: (*dense_tile(s), 0)),
                   pl.BlockSpec((1, KV_WIDTH, WINDOW), lambda s: (mixer_tile(s)[0], 0, 0)),
                   pl.BlockSpec((1, KV_WIDTH, WINDOW), lambda s: (mixer_tile(s)[0], 0, 0)),
                   pl.BlockSpec((POOL_HIST, B, pool_w), lambda s: (0, 0, 0)),
                   pl.BlockSpec((NS, 1, D), lambda s: (0, 0, 0)),
                   cache_blk, cache_blk],
        out_shape=[jax.ShapeDtypeStruct((B, S, D), F32),
                   jax.ShapeDtypeStruct((B, KV_WIDTH, WINDOW), F32),
                   jax.ShapeDtypeStruct((B, KV_WIDTH, WINDOW), F32),
                   jax.ShapeDtypeStruct((POOL_HIST, B, pool_w), F32),
                   jax.ShapeDtypeStruct((NS, 1, D), F32),
                   jax.ShapeDtypeStruct((NS, KV_WIDTH, L), F32),
                   jax.ShapeDtypeStruct((NS, KV_WIDTH, L), F32)],
        scratch_shapes=[pltpu.VMEM((D, in_w), BF16),
                        pltpu.VMEM((len(POOL_WINDOWS) // 2, 2 * LANES, 2 * LANES), BF16),
                        pltpu.VMEM((mixed_w, D), BF16),
                        pltpu.VMEM((D, d_ff), BF16),
                        pltpu.VMEM((D, d_ff), BF16),
                        pltpu.VMEM((d_ff, D), BF16),
                        pltpu.VMEM(w_pool.shape[1:], F32),
                        pltpu.SemaphoreType.DMA((n_units + 1,)),
                        pltpu.VMEM((KV_WIDTH, T), BF16),
                        pltpu.VMEM((KV_WIDTH, BLOCK), BF16),
                        pltpu.VMEM((4, T + BLOCK, LANES), BF16),
                        pltpu.VMEM((T + HIST_PAD, pool_w), F32),
                        pltpu.VMEM((T, mixed_w), BF16),
                        pltpu.VMEM((T, d_ff), BF16),
                        pltpu.VMEM((n_units, 2 * BLOCK, 4 * BLOCK), F32),
                        pltpu.VMEM((n_units, 2, 2 * BLOCK, LANES), F32),
                        pltpu.VMEM((n_units, 2 * BLOCK, 4 * BLOCK), BF16)],
        compiler_params=pltpu.CompilerParams(dimension_semantics=("arbitrary",),
                                             vmem_limit_bytes=VMEM_LIMIT_BYTES),
        name="prompt_layer",
    )(x_prompt, x_prompt, xs, mix_sample, n1, jnp.asarray(_prompt_bias()), sinks, pscale, n2, fn,
      kt_cache, vt_cache, knew_t, vnew_t, w_in, w_pool, w_out, w_gate, w_up, w_down)

    return (y_prompt, y_sample, cache_from_kernel(k_last), cache_from_kernel(v_last),
            jnp.transpose(pool_last, (1, 0, 2))[None], cache_from_kernel(nkt), cache_from_kernel(nvt),
            jnp.transpose(npool, (1, 0, 2))[None])
```
